```python
import math
import jax, jax.numpy as jnp
from jax import lax
import numpy as np

D_MODEL = 1024
BATCH = 32
SEQ = 2048
DEPTH = 1

N_MEM = 256
MLSTM_HEADS = 4
MLSTM_HEAD_DIM = D_MODEL // 8
SB_HEADS = 4
SB_HEAD_DIM = D_MODEL // 16
MEM_HEADS = 4
MEM_HEAD_DIM = D_MODEL // 16
D_MLSTM = MLSTM_HEADS * MLSTM_HEAD_DIM
D_SB = SB_HEADS * SB_HEAD_DIM
D_MEMX = MEM_HEADS * MEM_HEAD_DIM
D_MIX = D_MLSTM + D_SB + D_MEMX
CONV_WIDTH = 4
CHUNK = 128
Q_BLOCK = 128
EPS = 1e-6
SPLIT_SIZES = (D_MLSTM,) * 5 + (MLSTM_HEADS, MLSTM_HEADS) + (D_SB,) * 4 + (D_MEMX,) * 2
D_IN = sum(SPLIT_SIZES)

kernel_name = "hybrid_mlstm_stickbreak_memxattn_layer"


def rmsnorm(x, g):
    xf = x.astype(jnp.float32)
    y = xf * lax.rsqrt(jnp.mean(xf * xf, axis=-1, keepdims=True) + EPS)
    return (y * g.astype(jnp.float32)).astype(x.dtype)


def head_rmsnorm(h, g):
    y = h * lax.rsqrt(jnp.mean(h * h, axis=-1, keepdims=True) + EPS)
    return y * g.astype(jnp.float32).reshape(h.shape[-2:])


def split_heads(a, n):
    return a.reshape(a.shape[:-1] + (n, a.shape[-1] // n))


def causal_dwconv(x, w, b):
    c = x.shape[-1]
    y = lax.conv_general_dilated(x, w[:, None, :], window_strides=(1,),
                                 padding=[(w.shape[0] - 1, 0)],
                                 dimension_numbers=("NWC", "WIO", "NWC"),
                                 feature_group_count=c)
    return y + b


def mlstm_chunkwise(q, k, v, i_pre, f_pre):
    B, S, H, d = q.shape
    nc = S // CHUNK
    k = k * (d ** -0.5)
    logf = jax.nn.log_sigmoid(f_pre)

    def chunks4(a):
        return a.reshape(B, nc, CHUNK, H, a.shape[-1]).transpose(1, 0, 3, 2, 4)

    def chunks3(a):
        return a.reshape(B, nc, CHUNK, H).transpose(1, 0, 3, 2)

    mask = jnp.tril(jnp.ones((CHUNK, CHUNK), dtype=bool))

    def step(carry, inp):
        C, n, m = carry
        qb, kb, vb, ib, fb = inp
        b = jnp.cumsum(fb, axis=-1)
        dmat = b[..., :, None] - b[..., None, :] + ib[..., None, :]
        dmat = jnp.where(mask, dmat, -jnp.inf)
        inter = b + m[..., None]
        m_t = jnp.maximum(inter, jnp.max(dmat, axis=-1))
        dexp = jnp.exp(dmat - m_t[..., None])
        inter_w = jnp.exp(inter - m_t)
        s = jnp.einsum("bhtd,bhsd->bhts", qb, kb) * dexp
        num = (jnp.einsum("bhts,bhse->bhte", s, vb)
               + inter_w[..., None] * jnp.einsum("bhtd,bhde->bhte", qb, C))
        den = jnp.sum(s, axis=-1) + inter_w * jnp.einsum("bhtd,bhd->bht", qb, n)
        h = num / jnp.maximum(jnp.abs(den), jnp.exp(-m_t))[..., None]
        b_last = b[..., -1]
        a_end = b_last[..., None] - b + ib
        m_new = jnp.maximum(b_last + m, jnp.max(a_end, axis=-1))
        w_end = jnp.exp(a_end - m_new[..., None])
        decay = jnp.exp(b_last + m - m_new)
        C_new = decay[..., None, None] * C + jnp.einsum("bhs,bhsd,bhse->bhde", w_end, kb, vb)
        n_new = decay[..., None] * n + jnp.einsum("bhs,bhsd->bhd", w_end, kb)
        return (C_new, n_new, m_new), h

    init = (jnp.zeros((B, H, d, v.shape[-1]), jnp.float32),
            jnp.zeros((B, H, d), jnp.float32),
            jnp.zeros((B, H), jnp.float32))
    _, hs = lax.scan(step, init, (chunks4(q), chunks4(k), chunks4(v), chunks3(i_pre), chunks3(logf)))
    return hs.transpose(1, 0, 3, 2, 4).reshape(B, S, H, v.shape[-1])


def stick_breaking_attention(q, k, v):
    B, S, H, d = q.shape
    scale = d ** -0.5
    outs = []
    for blk in range(S // Q_BLOCK):
        t0 = blk * Q_BLOCK
        t1 = t0 + Q_BLOCK
        qb, kb, vb = q[:, t0:t1], k[:, :t1], v[:, :t1]
        z = jnp.einsum("bthd,bshd->bhts", qb, kb) * scale
        strict = jnp.arange(t1)[None, :] < (t0 + jnp.arange(Q_BLOCK))[:, None]
        sp = jax.nn.softplus(z)
        log_beta = z - sp
        log_1m = jnp.where(strict, -sp, 0.0)
        suffix = lax.cumsum(log_1m, axis=3, reverse=True) - log_1m
        a = jnp.where(strict, jnp.exp(log_beta + suffix), 0.0)
        outs.append(jnp.einsum("bhts,bshd->bthd", a, vb))
    return jnp.concatenate(outs, axis=1)


def memory_cross_attention(q, k, v):
    scores = jnp.einsum("bthd,bmhd->bhtm", q, k) * (q.shape[-1] ** -0.5)
    p = jax.nn.softmax(scores, axis=-1)
    return jnp.einsum("bhtm,bmhd->bthd", p, v)


def setup_inputs(seed: int = 0) -> dict:
    key = jax.random.key(seed)
    ks = jax.random.split(key, 16)
    f32 = jnp.float32
    nrm = lambda k, s: jax.random.normal(k, s, f32)
    return {
        "x": nrm(ks[0], (BATCH, SEQ, D_MODEL)),
        "mem": nrm(ks[1], (BATCH, N_MEM, D_MODEL)),
        "g_pre": 1.0 + 0.05 * nrm(ks[2], (D_MODEL,)),
        "w_in": nrm(ks[3], (D_MODEL, D_IN)) * D_MODEL ** -0.5,
        "conv_w": nrm(ks[4], (CONV_WIDTH, 2 * D_MLSTM)) * CONV_WIDTH ** -0.5,
        "conv_b": 0.01 * nrm(ks[5], (2 * D_MLSTM,)),
        "i_bias": 0.1 * nrm(ks[6], (MLSTM_HEADS,)),
        "f_bias": jnp.linspace(3.0, 6.0, MLSTM_HEADS, dtype=f32) + 0.1 * nrm(ks[7], (MLSTM_HEADS,)),
        "g_mlstm_head": 1.0 + 0.05 * nrm(ks[8], (D_MLSTM,)),
        "g_sb_head": 1.0 + 0.05 * nrm(ks[9], (D_SB,)),
        "g_mem_head": 1.0 + 0.05 * nrm(ks[10], (D_MEMX,)),
        "g_mem": 1.0 + 0.05 * nrm(ks[11], (D_MODEL,)),
        "w_mem_kv": nrm(ks[12], (D_MODEL, 2 * D_MEMX)) * D_MODEL ** -0.5,
        "w_out": nrm(ks[13], (D_MIX, D_MODEL)) * D_MIX ** -0.5,
        "g_post": 1.0 + 0.05 * nrm(ks[14], (D_MODEL,)),
    }


def reference(x, mem, g_pre, w_in, conv_w, conv_b, i_bias, f_bias, g_mlstm_head, g_sb_head,
              g_mem_head, g_mem, w_mem_kv, w_out, g_post):
    f32 = jnp.float32
    dt = x.dtype
    B, S, _ = x.shape
    split_idx = [int(i) for i in np.cumsum(SPLIT_SIZES)[:-1]]
    for _layer in range(DEPTH):
        h = rmsnorm(x, g_pre)
        u = h @ w_in
        (q_m, k_m, v_m, o_m, z_m, i_m, f_m,
         q_s, k_s, v_s, z_s, q_c, z_c) = jnp.split(u, split_idx, axis=-1)

        qk = causal_dwconv(jnp.concatenate([q_m, k_m], axis=-1).astype(f32),
                           conv_w.astype(f32), conv_b.astype(f32))
        q_m2, k_m2 = jnp.split(jax.nn.silu(qk), 2, axis=-1)
        h_m = mlstm_chunkwise(split_heads(q_m2, MLSTM_HEADS), split_heads(k_m2, MLSTM_HEADS),
                              split_heads(v_m.astype(f32), MLSTM_HEADS),
                              i_m.astype(f32) + i_bias.astype(f32),
                              f_m.astype(f32) + f_bias.astype(f32))
        h_m = jax.nn.sigmoid(o_m.astype(f32)) * head_rmsnorm(h_m, g_mlstm_head).reshape(B, S, D_MLSTM)
        y_m = h_m * jax.nn.silu(z_m.astype(f32))

        h_s = stick_breaking_attention(split_heads(q_s.astype(f32), SB_HEADS),
                                       split_heads(k_s.astype(f32), SB_HEADS),
                                       split_heads(v_s.astype(f32), SB_HEADS))
        y_s = head_rmsnorm(h_s, g_sb_head).reshape(B, S, D_SB) * jax.nn.silu(z_s.astype(f32))

        kv = rmsnorm(mem, g_mem) @ w_mem_kv
        k_c, v_c = jnp.split(kv.astype(f32), 2, axis=-1)
        h_c = memory_cross_attention(split_heads(q_c.astype(f32), MEM_HEADS),
                                     split_heads(k_c, MEM_HEADS), split_heads(v_c, MEM_HEADS))
        y_c = head_rmsnorm(h_c, g_mem_head).reshape(B, S, D_MEMX) * jax.nn.silu(z_c.astype(f32))

        y = jnp.concatenate([y_m, y_s, y_c], axis=-1).astype(dt) @ w_out
        x = x + rmsnorm(y, g_post)
    return x
```

```python
import functools

import jax
import jax.numpy as jnp
from jax import lax
from jax.experimental import pallas as pl
from jax.experimental.pallas import tpu as pltpu

F32 = jnp.float32
BF16 = jnp.bfloat16
EPS = 1e-6
LANES = 128
SUBLANES = 8
VMEM_LIMIT = 56 * 2 ** 20

MLSTM_HEADS = 4
SB_HEADS = 4
MEM_HEADS = 4
CONV_WIDTH = 4
CHUNK = 128
ROW_TILE = 256


def _sigmoid(x):
    return 1.0 / (1.0 + jnp.exp(-x))


def _silu(x):
    return x * _sigmoid(x)


def _softplus(x):
    return jnp.maximum(x, 0.0) + jnp.log(1.0 + jnp.exp(-jnp.abs(x)))


def _dot(a, b):
    return jnp.dot(a, b, preferred_element_type=F32)


def _dot_nt(a, b):
    return lax.dot_general(a, b, (((1,), (1,)), ((), ())), preferred_element_type=F32)


def _dot_tn(a, b):
    return lax.dot_general(a, b, (((0,), (0,)), ((), ())), preferred_element_type=F32)


def _split_bf16(x, parts):
    out, rem = [], x
    for _ in range(parts):
        p = rem.astype(BF16)
        out.append(p)
        rem = rem - p.astype(F32)
    return out


def _rms(x, axis=-1):
    return x * lax.rsqrt(jnp.mean(x * x, axis=axis, keepdims=True) + EPS)


def _inproj_kernel(x_ref, g_ref, w_ref, wg_ref, u_ref, gate_ref):
    h = (_rms(x_ref[...]) * g_ref[...]).astype(BF16)
    u_ref[...] = _dot(h, w_ref[...])
    gate_ref[...] = _dot(h, wg_ref[...])


def _memkv_kernel(mem_ref, g_ref, w_ref, kv_ref):
    h = (_rms(mem_ref[0]) * g_ref[...]).astype(BF16)
    kv_ref[0] = _dot(h, w_ref[...])


def _mlstm_kernel(qk_ref, v_ref, o_ref, z_ref, gate_ref, cw_ref, cb_ref, gb_ref, gh_ref,
                  y_ref, xbuf, c_state, n_state, m_state, *, ts, heads, dh):
    dm = heads * dh
    pad = SUBLANES

    @pl.when(pl.program_id(1) == 0)
    def _():
        xbuf[0:pad, :] = jnp.zeros((pad, 2 * dm), F32)
        c_state[...] = jnp.zeros_like(c_state)
        n_state[...] = jnp.zeros_like(n_state)
        m_state[...] = jnp.zeros_like(m_state)

    xbuf[pad:pad + ts, :] = qk_ref[...]
    acc = None
    for j in range(CONV_WIDTH):
        off = pad - (CONV_WIDTH - 1) + j
        term = xbuf[off:off + ts, :] * cw_ref[j:j + 1, :]
        acc = term if acc is None else acc + term
    qk = _silu(acc + cb_ref[...])
    xbuf[0:pad, :] = xbuf[ts:ts + pad, :]

    lane = lax.broadcasted_iota(jnp.int32, (CHUNK, LANES), 1)
    row = lax.broadcasted_iota(jnp.int32, (CHUNK, CHUNK), 0)
    col = lax.broadcasted_iota(jnp.int32, (CHUNK, CHUNK), 1)
    tril = row >= col
    tril_bf = jnp.where(tril, 1.0, 0.0).astype(BF16)

    for c in range(ts // CHUNK):
        r0 = c * CHUNK
        g = gate_ref[r0:r0 + CHUNK, :] + gb_ref[...]
        logf = -_softplus(-g)
        cum = _dot(tril_bf, jnp.concatenate(_split_bf16(logf, 3), axis=1))
        bcum = cum[:, :LANES] + cum[:, LANES:2 * LANES] + cum[:, 2 * LANES:]
        p = jnp.where(lane < heads, g, bcum)
        pt = p.T
        for h in range(heads):
            hs = slice(h * dh, (h + 1) * dh)
            qh = qk[r0:r0 + CHUNK, hs]
            kh = qk[r0:r0 + CHUNK, dm + h * dh:dm + (h + 1) * dh] * (dh ** -0.5)
            qb, kb = qh.astype(BF16), kh.astype(BF16)
            vb = v_ref[r0:r0 + CHUNK, hs].astype(BF16)
            bcol, icol = p[:, heads + h:heads + h + 1], p[:, h:h + 1]
            brow, irow = pt[heads + h:heads + h + 1, :], pt[h:h + 1, :]
            m_prev = m_state[h:h + 1, 0:1]
            c_prev = c_state[h]
            n_prev = n_state[h:h + 1, :]

            dmat = jnp.where(tril, bcol - brow + irow, -jnp.inf)
            inter = bcol + m_prev
            m_t = jnp.maximum(inter, jnp.max(dmat, axis=-1, keepdims=True))
            inter_w = jnp.exp(inter - m_t)
            s = _dot_nt(qb, kb) * jnp.exp(dmat - m_t)
            num = _dot(s.astype(BF16), vb) + inter_w * _dot(qb, c_prev.astype(BF16))
            den = (jnp.sum(s, axis=-1, keepdims=True)
                   + inter_w * jnp.sum(qh * n_prev, axis=-1, keepdims=True))
            hh = num / jnp.maximum(jnp.abs(den), jnp.exp(-m_t))

            b_last = bcol[CHUNK - 1:CHUNK, :]
            a_end = b_last - bcol + icol
            m_new = jnp.maximum(b_last + m_prev, jnp.max(a_end, axis=0, keepdims=True))
            kw = kh * jnp.exp(a_end - m_new)
            decay = jnp.exp(b_last + m_prev - m_new)
            c_state[h] = decay * c_prev + _dot_tn(kw.astype(BF16), vb)
            n_state[h:h + 1, :] = decay * n_prev + jnp.sum(kw, axis=0, keepdims=True)
            m_state[h:h + 1, :] = jnp.broadcast_to(m_new, (1, LANES))

            hn = _rms(hh) * gh_ref[:, hs]
            y_ref[r0:r0 + CHUNK, hs] = (_sigmoid(o_ref[r0:r0 + CHUNK, hs]) * hn
                                        * _silu(z_ref[r0:r0 + CHUNK, hs]))


def _sb_kernel(q_ref, k_ref, v_ref, z_ref, g_ref, y_ref, *, tq, heads, dh):
    i = pl.program_id(1)
    row = lax.broadcasted_iota(jnp.int32, (tq, tq), 0)
    col = lax.broadcasted_iota(jnp.int32, (tq, tq), 1)
    strict = col < row
    jj = lax.broadcasted_iota(jnp.int32, (2 * tq, tq), 0)
    ss = lax.broadcasted_iota(jnp.int32, (2 * tq, tq), 1)
    uu = jnp.where(jnp.where(jj >= tq, jj - tq, jj) > ss, 1.0, 0.0).astype(BF16)

    qb = (q_ref[...] * (dh ** -0.5)).astype(BF16)

    def tile(j, state, masked):
        k0 = pl.multiple_of(j * tq, tq)
        kb = k_ref[pl.ds(k0, tq), :].astype(BF16)
        vb = v_ref[pl.ds(k0, tq), :].astype(BF16)
        new = []
        for h in range(heads):
            hs = slice(h * dh, (h + 1) * dh)
            carry, acc = state[h]
            z = _dot_nt(qb[:, hs], kb[:, hs])
            sp = _softplus(z)
            log1m = jnp.where(strict, -sp, 0.0) if masked else -sp
            hi, lo = _split_bf16(log1m, 2)
            suffix = _dot(jnp.concatenate([hi, lo], axis=1), uu)
            a = jnp.exp(z - sp + suffix + carry)
            if masked:
                a = jnp.where(strict, a, 0.0)
            acc = acc + _dot(a.astype(BF16), vb[:, hs])
            carry = carry + jnp.sum(log1m, axis=-1, keepdims=True)
            new.append((carry, acc))
        return tuple(new)

    state = tuple((jnp.zeros((tq, 1), F32), jnp.zeros((tq, dh), F32)) for _ in range(heads))
    state = tile(i, state, True)
    state = lax.fori_loop(0, i, lambda n, st: tile(i - 1 - n, st, False), state)

    hn = jnp.concatenate([_rms(acc) for _, acc in state], axis=1)
    y_ref[...] = hn * g_ref[...] * _silu(z_ref[...])


def _out_kernel(x_ref, ym_ref, ys_ref, qc_ref, zc_ref, kv_ref, gmh_ref, wo_ref, gp_ref,
                out_ref, *, heads, dh, dm, ds):
    dc = heads * dh
    qb = (qc_ref[...] * (dh ** -0.5)).astype(BF16)
    kv = kv_ref[0].astype(BF16)
    outs = []
    for h in range(heads):
        hs = slice(h * dh, (h + 1) * dh)
        sc = _dot_nt(qb[:, hs], kv[:, hs])
        pr = jnp.exp(sc - jnp.max(sc, axis=-1, keepdims=True))
        o = _dot(pr.astype(BF16), kv[:, dc + h * dh:dc + (h + 1) * dh])
        outs.append(_rms(o / jnp.sum(pr, axis=-1, keepdims=True)))
    yc = jnp.concatenate(outs, axis=1) * gmh_ref[...] * _silu(zc_ref[...])
    y = (_dot(ym_ref[...].astype(BF16), wo_ref[0:dm, :])
         + _dot(ys_ref[...].astype(BF16), wo_ref[dm:dm + ds, :])
         + _dot(yc.astype(BF16), wo_ref[dm + ds:, :]))
    out_ref[...] = x_ref[...] + _rms(y) * gp_ref[...]


def _params(*sem):
    return pltpu.CompilerParams(dimension_semantics=sem, vmem_limit_bytes=VMEM_LIMIT)


def _col(width, offset):
    assert offset % width == 0
    return offset // width


def kernel(x, mem, g_pre, w_in, conv_w, conv_b, i_bias, f_bias, g_mlstm_head, g_sb_head,
           g_mem_head, g_mem, w_mem_kv, w_out, g_post):
    B, S, D = x.shape
    M = mem.shape[1]
    dm, ds, dc = g_mlstm_head.shape[0], g_sb_head.shape[0], g_mem_head.shape[0]
    hm, hsb, hc = MLSTM_HEADS, SB_HEADS, MEM_HEADS
    ts = ROW_TILE
    assert S % ts == 0 and ts % CHUNK == 0 and 2 * hm <= LANES
    nt = S // ts
    n_main = 5 * dm + 4 * ds + 2 * dc
    gate0 = 5 * dm

    w_main = jnp.concatenate([w_in[:, :gate0], w_in[:, gate0 + 2 * hm:]], axis=1).astype(BF16)
    w_gate = jnp.pad(w_in[:, gate0:gate0 + 2 * hm], ((0, 0), (0, LANES - 2 * hm))).astype(BF16)
    gate_bias = jnp.pad(jnp.concatenate([i_bias, f_bias]), (0, LANES - 2 * hm)).reshape(1, LANES)
    row2 = lambda a: a.reshape(1, -1)

    x2 = x.reshape(B * S, D)
    const = lambda *_: (0, 0)

    u, gates = pl.pallas_call(
        _inproj_kernel,
        grid=(B * nt,),
        in_specs=[pl.BlockSpec((ts, D), lambda r: (r, 0)),
                  pl.BlockSpec((1, D), const),
                  pl.BlockSpec((D, n_main), const),
                  pl.BlockSpec((D, LANES), const)],
        out_specs=[pl.BlockSpec((ts, n_main), lambda r: (r, 0)),
                   pl.BlockSpec((ts, LANES), lambda r: (r, 0))],
        out_shape=[jax.ShapeDtypeStruct((B * S, n_main), F32),
                   jax.ShapeDtypeStruct((B * S, LANES), F32)],
        compiler_params=_params("arbitrary"),
        name="in_proj",
    )(x2, row2(g_pre), w_main, w_gate)

    kv = pl.pallas_call(
        _memkv_kernel,
        grid=(B,),
        in_specs=[pl.BlockSpec((1, M, D), lambda b: (b, 0, 0)),
                  pl.BlockSpec((1, D), const),
                  pl.BlockSpec((D, 2 * dc), const)],
        out_specs=pl.BlockSpec((1, M, 2 * dc), lambda b: (b, 0, 0)),
        out_shape=jax.ShapeDtypeStruct((B, M, 2 * dc), F32),
        compiler_params=_params("arbitrary"),
        name="mem_kv",
    )(mem, row2(g_mem), w_mem_kv.astype(BF16))

    def rows(width, offset):
        c = _col(width, offset)
        return pl.BlockSpec((ts, width), lambda b, t: (b * nt + t, c))

    y_m = pl.pallas_call(
        functools.partial(_mlstm_kernel, ts=ts, heads=hm, dh=dm // hm),
        grid=(B, nt),
        in_specs=[rows(2 * dm, 0), rows(dm, 2 * dm), rows(dm, 3 * dm), rows(dm, 4 * dm),
                  pl.BlockSpec((ts, LANES), lambda b, t: (b * nt + t, 0)),
                  pl.BlockSpec((CONV_WIDTH, 2 * dm), const),
                  pl.BlockSpec((1, 2 * dm), const),
                  pl.BlockSpec((1, LANES), const),
                  pl.BlockSpec((1, dm), const)],
        out_specs=pl.BlockSpec((ts, dm), lambda b, t: (b * nt + t, 0)),
        out_shape=jax.ShapeDtypeStruct((B * S, dm), F32),
        scratch_shapes=[pltpu.VMEM((ts + SUBLANES, 2 * dm), F32),
                        pltpu.VMEM((hm, dm // hm, dm // hm), F32),
                        pltpu.VMEM((SUBLANES, dm // hm), F32),
                        pltpu.VMEM((SUBLANES, LANES), F32)],
        compiler_params=_params("arbitrary", "arbitrary"),
        name="mlstm",
    )(u, u, u, u, gates, conv_w, row2(conv_b), gate_bias, row2(g_mlstm_head))

    sb0 = 5 * dm
    y_s = pl.pallas_call(
        functools.partial(_sb_kernel, tq=ts, heads=hsb, dh=ds // hsb),
        grid=(B, nt),
        in_specs=[rows(ds, sb0),
                  pl.BlockSpec((S, ds), lambda b, t: (b, _col(ds, sb0 + ds))),
                  pl.BlockSpec((S, ds), lambda b, t: (b, _col(ds, sb0 + 2 * ds))),
                  rows(ds, sb0 + 3 * ds),
                  pl.BlockSpec((1, ds), const)],
        out_specs=pl.BlockSpec((ts, ds), lambda b, t: (b * nt + t, 0)),
        out_shape=jax.ShapeDtypeStruct((B * S, ds), F32),
        compiler_params=_params("arbitrary", "arbitrary"),
        name="stick_breaking",
    )(u, u, u, u, row2(g_sb_head))

    mc0 = sb0 + 4 * ds
    out = pl.pallas_call(
        functools.partial(_out_kernel, heads=hc, dh=dc // hc, dm=dm, ds=ds),
        grid=(B, nt),
        in_specs=[pl.BlockSpec((ts, D), lambda b, t: (b * nt + t, 0)),
                  pl.BlockSpec((ts, dm), lambda b, t: (b * nt + t, 0)),
                  pl.BlockSpec((ts, ds), lambda b, t: (b * nt + t, 0)),
                  rows(dc, mc0), rows(dc, mc0 + dc),
                  pl.BlockSpec((1, M, 2 * dc), lambda b, t: (b, 0, 0)),
                  pl.BlockSpec((1, dc), const),
                  pl.BlockSpec((dm + ds + dc, D), const),
                  pl.BlockSpec((1, D), const)],
        out_specs=pl.BlockSpec((ts, D), lambda b, t: (b * nt + t, 0)),
        out_shape=jax.ShapeDtypeStruct((B * S, D), x.dtype),
        compiler_params=_params("arbitrary", "arbitrary"),
        name="mem_attn_out_proj",
    )(x2, y_m, y_s, u, u, kv, row2(g_mem_head), w_out.astype(BF16), row2(g_post))

    return out.reshape(B, S, D)
```

```python
import functools
import math

import jax
import jax.numpy as jnp
from jax import lax
from jax.experimental import pallas as pl
from jax.experimental.pallas import tpu as pltpu

F32 = jnp.float32
BF16 = jnp.bfloat16
EPS = 1e-6
LANES = 128
SUBLANES = 8
VMEM_LIMIT = 56 * 2 ** 20

MLSTM_HEADS = 4
SB_HEADS = 4
MEM_HEADS = 4
CONV_WIDTH = 4
CHUNK = 128
ROW_TILE = 256
SB_QTILE = 512
SB_KTILE = 256
SB_STRIP = 32
LOG2E = 1.4426950408889634


def _sigmoid(x):
    return 1.0 / (1.0 + jnp.exp(-x))


def _silu(x):
    return x * _sigmoid(x)


def _softplus(x):
    return jnp.maximum(x, 0.0) + jnp.log(1.0 + jnp.exp(-jnp.abs(x)))


def _dot(a, b):
    return jnp.dot(a, b, preferred_element_type=F32)


def _dot_nt(a, b):
    return lax.dot_general(a, b, (((1,), (1,)), ((), ())), preferred_element_type=F32)


def _split_bf16(x, parts):
    out, rem = [], x
    for n in range(parts):
        p = rem.astype(BF16)
        out.append(p)
        if n + 1 < parts:
            rem = rem - p.astype(F32)
    return out


def _rms(x, axis=-1):
    return x * lax.rsqrt(jnp.mean(x * x, axis=axis, keepdims=True) + EPS)


def _inproj_kernel(x_ref, g_ref, w32_ref, w16_ref, wg_ref, u32_ref, u16_ref, gate_ref):
    h = (_rms(x_ref[...]) * g_ref[...]).astype(BF16)
    u32_ref[...] = _dot(h, w32_ref[...])
    u16_ref[...] = _dot(h, w16_ref[...]).astype(BF16)
    gate_ref[...] = _dot(h, wg_ref[...])


def _memkv_kernel(mem_ref, g_ref, w_ref, kv_ref):
    h = (_rms(mem_ref[0]) * g_ref[...]).astype(BF16)
    kv_ref[0] = _dot(h, w_ref[...]).astype(BF16)


def _mlstm_kernel(qk_ref, v_ref, o_ref, z_ref, gate_ref, cw_ref, cb_ref, gb_ref, gh_ref,
                  y_ref, xbuf, ct_state, m_state, *, ts, heads, dh):
    dm = heads * dh
    pad = SUBLANES
    ext = ct_state.shape[1]

    @pl.when(pl.program_id(1) == 0)
    def _():
        xbuf[0:pad, :] = jnp.zeros((pad, 2 * dm), F32)
        ct_state[...] = jnp.zeros_like(ct_state)
        m_state[...] = jnp.zeros_like(m_state)

    xbuf[pad:pad + ts, :] = qk_ref[...]
    acc = None
    for j in range(CONV_WIDTH):
        off = pad - (CONV_WIDTH - 1) + j
        term = xbuf[off:off + ts, :] * cw_ref[j:j + 1, :]
        acc = term if acc is None else acc + term
    qk = _silu(acc + cb_ref[...])
    xbuf[0:pad, :] = xbuf[ts:ts + pad, :]

    lane = lax.broadcasted_iota(jnp.int32, (CHUNK, LANES), 1)
    row = lax.broadcasted_iota(jnp.int32, (CHUNK, CHUNK), 0)
    col = lax.broadcasted_iota(jnp.int32, (CHUNK, CHUNK), 1)
    tril_bf = jnp.where(row >= col, 1.0, 0.0).astype(BF16)
    upper = row <= col
    ones_col = jnp.where(lane == 0, 1.0, 0.0).astype(BF16)

    for c in range(ts // CHUNK):
        r0 = c * CHUNK
        g = gate_ref[r0:r0 + CHUNK, :] + gb_ref[...]
        logf = -_softplus(-g)
        cum = _dot(tril_bf, jnp.concatenate(_split_bf16(logf, 3), axis=1))
        bcum = cum[:, :LANES] + cum[:, LANES:2 * LANES] + cum[:, 2 * LANES:]
        p = jnp.where(lane < heads, g, bcum)
        pt = p.T
        hsl = [slice(h * dh, (h + 1) * dh) for h in range(heads)]
        t = [dict() for _ in range(heads)]

        def operands(h):
            e = t[h]
            e["kb"] = (qk[r0:r0 + CHUNK, dm + h * dh:dm + (h + 1) * dh] * (dh ** -0.5)).astype(BF16)
            e["qt"] = qk[r0:r0 + CHUNK, hsl[h]].astype(BF16).T
            vext = jnp.concatenate([v_ref[r0:r0 + CHUNK, hsl[h]], ones_col], axis=1)
            e["vt"] = vext.T[0:ext, :]
            e["kq"] = _dot(e["kb"], e["qt"])
            e["ct"] = ct_state[h]
            e["cq"] = _dot(e["ct"].astype(BF16), e["qt"])

        def decay_matrix(h):
            e = t[h]
            e["brow"], e["irow"] = pt[heads + h:heads + h + 1, :], pt[h:h + 1, :]
            cvec = p[:, h:h + 1] - p[:, heads + h:heads + h + 1]
            e["m_prev"] = m_state[h:h + 1, 0:1]
            dt = jnp.where(upper, cvec + e["brow"], -jnp.inf)
            inter = e["brow"] + e["m_prev"]
            e["m_t"] = jnp.maximum(inter, jnp.max(dt, axis=0, keepdims=True))
            e["inter_w"] = jnp.exp(inter - e["m_t"])
            st = e.pop("kq") * jnp.exp(dt - e["m_t"])
            e["sv"] = _dot(e["vt"], st.astype(BF16))

        def state_update(h):
            e = t[h]
            b_last = e["brow"][:, CHUNK - 1:CHUNK]
            a_end = b_last - e["brow"] + e["irow"]
            m_new = jnp.maximum(b_last + e["m_prev"], jnp.max(a_end, axis=1, keepdims=True))
            vw = (e["vt"] * jnp.exp(a_end - m_new)).astype(BF16)
            decay = jnp.exp(b_last + e["m_prev"] - m_new)
            ct_state[h] = decay * e["ct"] + _dot(vw, e["kb"])
            m_state[h:h + 1, :] = jnp.broadcast_to(m_new, (1, LANES))

        def output(h):
            e = t[h]
            ne = e["sv"] + e["inter_w"] * e["cq"]
            den = ne[dh:dh + 1, :]
            ht = ne[0:dh, :] / jnp.maximum(jnp.abs(den), jnp.exp(-e["m_t"]))
            hn = _rms(ht, axis=0).T * gh_ref[:, hsl[h]]
            y_ref[r0:r0 + CHUNK, hsl[h]] = (_sigmoid(o_ref[r0:r0 + CHUNK, hsl[h]]) * hn
                                            * _silu(z_ref[r0:r0 + CHUNK, hsl[h]]))
            e.clear()

        for stage in (operands, decay_matrix, state_update, output):
            for h in range(heads):
                stage(h)


def _sb_kernel(q_ref, k_ref, v_ref, z_ref, g_ref, y_ref, acc_ref, carry_ref, pend_ref,
               *, tq, tk, heads, dh):
    i = pl.program_id(1)
    nsub = tq // tk
    jj = lax.broadcasted_iota(jnp.int32, (2 * tk, tk), 0)
    ss = lax.broadcasted_iota(jnp.int32, (2 * tk, tk), 1)
    uu = jnp.where(jnp.where(jj >= tk, jj - tk, jj) >= ss, 1.0, 0.0).astype(BF16)
    strict = (lax.broadcasted_iota(jnp.int32, (tk, tk), 1)
              < lax.broadcasted_iota(jnp.int32, (tk, tk), 0))
    hs = [slice(h * dh, (h + 1) * dh) for h in range(heads)]
    strips = range(0, tk, SB_STRIP)

    def sweep(base, pairs, fresh, pending, drain):
        items = [(sub, koff, masked, h) for sub, koff, masked in pairs for h in range(heads)]
        z, hl, c = {}, {}, {}
        if pending:
            items = items + [(nsub - 1, 1, False, heads - 2), (nsub - 1, 1, False, heads - 1)]
            z[-2], c[-2], z[-1] = pend_ref[0], pend_ref[1], pend_ref[2]

        def kv_tile(ref, koff, h):
            k0 = (base + koff) * tk
            k0 = k0 if isinstance(k0, int) else pl.multiple_of(k0, tk)
            return ref[pl.ds(k0, tk), hs[h]]

        def scores(n):
            sub, koff, _, h = items[n]
            z[n] = _dot_nt(q_ref[sub * tk:(sub + 1) * tk, hs[h]], kv_tile(k_ref, koff, h))

        def softplus_split(n):
            parts = []
            for r in strips:
                zs = z[n][r:r + SB_STRIP]
                sp = jnp.maximum(zs, 0.0) + jnp.log(1.0 + jnp.exp2(jnp.abs(zs) * -LOG2E))
                if items[n][2]:
                    sp = jnp.where(strict[r:r + SB_STRIP], sp, 0.0)
                parts.append(jnp.concatenate(_split_bf16(sp, 2), axis=1))
            hl[n] = jnp.concatenate(parts, axis=0)

        def cumsum(n):
            c[n] = _dot(hl.pop(n), uu)

        def weights_times_v(n):
            sub, koff, masked, h = items[n]
            zn, cn = z.pop(n), c.pop(n)
            parts = []
            for r in strips:
                a = jnp.exp2((zn[r:r + SB_STRIP] - cn[r:r + SB_STRIP]) * LOG2E)
                if masked:
                    a = jnp.where(strict[r:r + SB_STRIP], a, 0.0)
                parts.append(a.astype(BF16))
            av = _dot(jnp.concatenate(parts, axis=0), kv_tile(v_ref, koff, h))
            slot = sub * heads + h
            if (sub, h) in fresh:
                fresh.discard((sub, h))
                carry_ref[slot] = cn[:, 0:1]
                acc_ref[slot] = av
            else:
                carry = carry_ref[slot]
                carry_ref[slot] = carry + cn[:, 0:1]
                acc_ref[slot] = acc_ref[slot] + jnp.exp2(carry * -LOG2E) * av

        n_new = len(items) - (2 if pending else 0)
        first = -2 if pending else 0
        for step in range(n_new + (2 if drain else 0)):
            if step < n_new:
                scores(step)
            if max(first, -1) <= step - 1 < n_new:
                softplus_split(step - 1)
                cumsum(step - 1)
            if first <= step - 2 < n_new:
                weights_times_v(step - 2)
        if not drain:
            pend_ref[0], pend_ref[1], pend_ref[2] = z[n_new - 2], c[n_new - 2], z[n_new - 1]

    diag = [(sub, koff, koff == sub) for sub in range(nsub) for koff in range(sub, -1, -1)]
    full = [(sub, 0, False) for sub in range(nsub)]
    fresh = {(sub, h) for sub in range(nsub) for h in range(heads)}
    sweep(i * nsub, diag, fresh, False, False)
    for sub, h in sorted(fresh):
        carry_ref[sub * heads + h] = jnp.zeros((tk, 1), F32)
        acc_ref[sub * heads + h] = jnp.zeros((tk, dh), F32)

    def body(n, _):
        sweep(i * nsub - 1 - n, full, set(), True, False)
        return 0

    lax.fori_loop(0, i * nsub, body, 0)
    sweep(-1, [], set(), True, True)

    for sub in range(nsub):
        hn = jnp.concatenate([_rms(acc_ref[sub * heads + h]) for h in range(heads)], axis=1)
        rows = slice(sub * tk, (sub + 1) * tk)
        y_ref[rows, :] = hn * g_ref[...] * _silu(z_ref[rows, :])


def _out_kernel(x_ref, ym_ref, ys_ref, qc_ref, zc_ref, kv_ref, gmh_ref, wo_ref, gp_ref,
                out_ref, *, heads, dh, dm, ds):
    dc = heads * dh
    outs = []
    for h in range(heads):
        hs = slice(h * dh, (h + 1) * dh)
        sc = _dot_nt(qc_ref[:, hs], kv_ref[0, :, hs])
        pr = jnp.exp(sc - jnp.max(sc, axis=-1, keepdims=True))
        o = _dot(pr.astype(BF16), kv_ref[0, :, dc + h * dh:dc + (h + 1) * dh])
        outs.append(_rms(o / jnp.sum(pr, axis=-1, keepdims=True)))
    yc = jnp.concatenate(outs, axis=1) * gmh_ref[...] * _silu(zc_ref[...])
    y = (_dot(ym_ref[...].astype(BF16), wo_ref[0:dm, :])
         + _dot(ys_ref[...].astype(BF16), wo_ref[dm:dm + ds, :])
         + _dot(yc.astype(BF16), wo_ref[dm + ds:, :]))
    out_ref[...] = x_ref[...] + _rms(y) * gp_ref[...]


def _params(*sem):
    return pltpu.CompilerParams(dimension_semantics=sem, vmem_limit_bytes=VMEM_LIMIT)


def _col(width, offset):
    assert offset % width == 0
    return offset // width


def kernel(x, mem, g_pre, w_in, conv_w, conv_b, i_bias, f_bias, g_mlstm_head, g_sb_head,
           g_mem_head, g_mem, w_mem_kv, w_out, g_post):
    B, S, D = x.shape
    M = mem.shape[1]
    dm, ds, dc = g_mlstm_head.shape[0], g_sb_head.shape[0], g_mem_head.shape[0]
    hm, hsb, hc = MLSTM_HEADS, SB_HEADS, MEM_HEADS
    ts = ROW_TILE
    assert S % ts == 0 and ts % CHUNK == 0 and 2 * hm <= LANES
    nt = S // ts

    cols, off = {}, 0
    for name, width in (("q_m", dm), ("k_m", dm), ("v_m", dm), ("o_m", dm), ("z_m", dm),
                        ("if", 2 * hm), ("q_s", ds), ("k_s", ds), ("v_s", ds), ("z_s", ds),
                        ("q_c", dc), ("z_c", dc)):
        cols[name] = w_in[:, off:off + width]
        off += width
    sb_scale, mem_scale = (ds // hsb) ** -0.5, (dc // hc) ** -0.5
    assert math.log2(sb_scale).is_integer() and math.log2(mem_scale).is_integer()
    w32 = jnp.concatenate([cols[n] for n in ("q_m", "k_m", "o_m", "z_m", "z_s", "z_c")], axis=1).astype(BF16)
    w16 = jnp.concatenate([cols["v_m"], cols["q_s"] * sb_scale, cols["k_s"], cols["v_s"],
                           cols["q_c"] * mem_scale], axis=1).astype(BF16)
    w_gate = jnp.pad(cols["if"], ((0, 0), (0, LANES - 2 * hm))).astype(BF16)
    gate_bias = jnp.pad(jnp.concatenate([i_bias, f_bias]), (0, LANES - 2 * hm)).reshape(1, LANES)
    n32, n16 = w32.shape[1], w16.shape[1]
    o32 = {"qk_m": 0, "o_m": 2 * dm, "z_m": 3 * dm, "z_s": 4 * dm, "z_c": 4 * dm + ds}
    o16 = {"v_m": 0, "q_s": dm, "k_s": dm + ds, "v_s": dm + 2 * ds, "q_c": dm + 3 * ds}
    row2 = lambda a: a.reshape(1, -1)

    x2 = x.reshape(B * S, D)
    const = lambda *_: (0, 0)

    def rows(width, offset):
        c = _col(width, offset)
        return pl.BlockSpec((ts, width), lambda b, t: (b * nt + t, c))

    u32, u16, gates = pl.pallas_call(
        _inproj_kernel,
        grid=(B, nt),
        in_specs=[rows(D, 0),
                  pl.BlockSpec((1, D), const),
                  pl.BlockSpec((D, n32), const),
                  pl.BlockSpec((D, n16), const),
                  pl.BlockSpec((D, LANES), const)],
        out_specs=[rows(n32, 0), rows(n16, 0), rows(LANES, 0)],
        out_shape=[jax.ShapeDtypeStruct((B * S, n32), F32),
                   jax.ShapeDtypeStruct((B * S, n16), BF16),
                   jax.ShapeDtypeStruct((B * S, LANES), F32)],
        compiler_params=_params("arbitrary", "arbitrary"),
        name="in_proj",
    )(x2, row2(g_pre), w32, w16, w_gate)

    kv = pl.pallas_call(
        _memkv_kernel,
        grid=(B,),
        in_specs=[pl.BlockSpec((1, M, D), lambda b: (b, 0, 0)),
                  pl.BlockSpec((1, D), const),
                  pl.BlockSpec((D, 2 * dc), const)],
        out_specs=pl.BlockSpec((1, M, 2 * dc), lambda b: (b, 0, 0)),
        out_shape=jax.ShapeDtypeStruct((B, M, 2 * dc), BF16),
        compiler_params=_params("arbitrary"),
        name="mem_kv",
    )(mem, row2(g_mem), w_mem_kv.astype(BF16))

    y_m = pl.pallas_call(
        functools.partial(_mlstm_kernel, ts=ts, heads=hm, dh=dm // hm),
        grid=(B, nt),
        in_specs=[rows(2 * dm, o32["qk_m"]), rows(dm, o16["v_m"]), rows(dm, o32["o_m"]),
                  rows(dm, o32["z_m"]), rows(LANES, 0),
                  pl.BlockSpec((CONV_WIDTH, 2 * dm), const),
                  pl.BlockSpec((1, 2 * dm), const),
                  pl.BlockSpec((1, LANES), const),
                  pl.BlockSpec((1, dm), const)],
        out_specs=rows(dm, 0),
        out_shape=jax.ShapeDtypeStruct((B * S, dm), F32),
        scratch_shapes=[pltpu.VMEM((ts + SUBLANES, 2 * dm), F32),
                        pltpu.VMEM((hm, dm // hm + 2 * SUBLANES, dm // hm), F32),
                        pltpu.VMEM((SUBLANES, LANES), F32)],
        compiler_params=_params("arbitrary", "arbitrary"),
        name="mlstm",
    )(u32, u16, u32, u32, gates, conv_w, row2(conv_b), gate_bias, row2(g_mlstm_head))

    tq = SB_QTILE
    assert S % tq == 0 and tq % SB_KTILE == 0 and SB_KTILE % SB_STRIP == 0
    ntq = S // tq

    def qrows(width, offset):
        c = _col(width, offset)
        return pl.BlockSpec((tq, width), lambda b, t: (b * ntq + t, c))

    y_s = pl.pallas_call(
        functools.partial(_sb_kernel, tq=tq, tk=SB_KTILE, heads=hsb, dh=ds // hsb),
        grid=(B, ntq),
        in_specs=[qrows(ds, o16["q_s"]),
                  pl.BlockSpec((S, ds), lambda b, t: (b, _col(ds, o16["k_s"]))),
                  pl.BlockSpec((S, ds), lambda b, t: (b, _col(ds, o16["v_s"]))),
                  qrows(ds, o32["z_s"]),
                  pl.BlockSpec((1, ds), const)],
        out_specs=qrows(ds, 0),
        out_shape=jax.ShapeDtypeStruct((B * S, ds), F32),
        scratch_shapes=[pltpu.VMEM((tq // SB_KTILE * hsb, SB_KTILE, ds // hsb), F32),
                        pltpu.VMEM((tq // SB_KTILE * hsb, SB_KTILE, 1), F32),
                        pltpu.VMEM((3, SB_KTILE, SB_KTILE), F32)],
        compiler_params=_params("arbitrary", "arbitrary"),
        name="stick_breaking",
    )(u16, u16, u16, u32, row2(g_sb_head))

    out = pl.pallas_call(
        functools.partial(_out_kernel, heads=hc, dh=dc // hc, dm=dm, ds=ds),
        grid=(B, nt),
        in_specs=[rows(D, 0), rows(dm, 0), rows(ds, 0),
                  rows(dc, o16["q_c"]), rows(dc, o32["z_c"]),
                  pl.BlockSpec((1, M, 2 * dc), lambda b, t: (b, 0, 0)),
                  pl.BlockSpec((1, dc), const),
                  pl.BlockSpec((dm + ds + dc, D), const),
                  pl.BlockSpec((1, D), const)],
        out_specs=rows(D, 0),
        out_shape=jax.ShapeDtypeStruct((B * S, D), x.dtype),
        compiler_params=_params("arbitrary", "arbitrary"),
        name="mem_attn_out_proj",
    )(x2, y_m, y_s, u16, u32, kv, row2(g_mem_head), w_out.astype(BF16), row2(g_post))

    return out.reshape(B, S, D)
```

```python
import functools
import math

import jax
import jax.numpy as jnp
from jax import lax
from jax.experimental import pallas as pl
from jax.experimental.pallas import tpu as pltpu

F32 = jnp.float32
BF16 = jnp.bfloat16
EPS = 1e-6
LANES = 128
SUBLANES = 8
VMEM_LIMIT = 56 * 2 ** 20

MLSTM_HEADS = 4
SB_HEADS = 4
MEM_HEADS = 4
CONV_WIDTH = 4
CHUNK = 128
ROW_TILE = 1024
PROJ_TILE = 512
SB_QTILE = 1024
SB_KTILE = 256
SB_STRIP = 32
SB_CUMSUM_TERMS = 1
LOG2E = 1.4426950408889634


def _sigmoid(x):
    return 1.0 / (1.0 + jnp.exp2(x * -LOG2E))


def _silu(x):
    return x * _sigmoid(x)


def _softplus(x):
    return jnp.maximum(x, 0.0) + jnp.log(1.0 + jnp.exp(-jnp.abs(x)))


def _dot(a, b):
    return jnp.dot(a, b, preferred_element_type=F32)


def _dot_nt(a, b):
    return lax.dot_general(a, b, (((1,), (1,)), ((), ())), preferred_element_type=F32)


def _split_bf16(x, parts):
    out, rem = [], x
    for n in range(parts):
        p = rem.astype(BF16)
        out.append(p)
        if n + 1 < parts:
            rem = rem - p.astype(F32)
    return out


def _rms(x, axis=-1):
    return x * lax.rsqrt(jnp.mean(x * x, axis=axis, keepdims=True) + EPS)


def _inproj_kernel(x_ref, g_ref, w32_ref, w16_ref, wg_ref, u32_ref, u16_ref, gate_ref):
    h = (_rms(x_ref[...]) * g_ref[...]).astype(BF16)
    u32_ref[...] = _dot(h, w32_ref[...])
    u16_ref[...] = _dot(h, w16_ref[...]).astype(BF16)
    gate_ref[...] = _dot(h, wg_ref[...])


def _memkv_kernel(mem_ref, g_ref, w_ref, k_ref, vt_ref):
    h = (_rms(mem_ref[0]) * g_ref[...]).astype(BF16)
    kv = _dot(h, w_ref[...])
    dc = kv.shape[1] // 2
    k_ref[0] = kv[:, :dc].astype(BF16)
    vt_ref[0] = kv[:, dc:].T.astype(BF16)


def _mlstm_kernel(qk_ref, v_ref, o_ref, z_ref, gate_ref, cw_ref, cb_ref, gb_ref, gh_ref,
                  y_ref, xbuf, ct_state, m_state, *, ts, heads, dh):
    dm = heads * dh
    pad = SUBLANES
    ext = ct_state.shape[1]

    @pl.when(pl.program_id(1) == 0)
    def _():
        xbuf[0:pad, :] = jnp.zeros((pad, 2 * dm), F32)
        ct_state[...] = jnp.zeros_like(ct_state)
        m_state[...] = jnp.zeros_like(m_state)

    xbuf[pad:pad + ts, :] = qk_ref[...]
    acc = None
    for j in range(CONV_WIDTH):
        off = pad - (CONV_WIDTH - 1) + j
        term = xbuf[off:off + ts, :] * cw_ref[j:j + 1, :]
        acc = term if acc is None else acc + term
    qk = _silu(acc + cb_ref[...])
    xbuf[0:pad, :] = xbuf[ts:ts + pad, :]

    lane = lax.broadcasted_iota(jnp.int32, (CHUNK, LANES), 1)
    row = lax.broadcasted_iota(jnp.int32, (CHUNK, CHUNK), 0)
    col = lax.broadcasted_iota(jnp.int32, (CHUNK, CHUNK), 1)
    tril_bf = jnp.where(row >= col, 1.0, 0.0).astype(BF16)
    upper = row <= col
    ones_col = jnp.where(lane == 0, 1.0, 0.0).astype(BF16)

    for c in range(ts // CHUNK):
        r0 = c * CHUNK
        g = gate_ref[r0:r0 + CHUNK, :] + gb_ref[...]
        logf = -_softplus(-g)
        cum = _dot(tril_bf, jnp.concatenate(_split_bf16(logf, 3), axis=1))
        bcum = cum[:, :LANES] + cum[:, LANES:2 * LANES] + cum[:, 2 * LANES:]
        p = jnp.where(lane < heads, g, bcum)
        pt = p.T
        hsl = [slice(h * dh, (h + 1) * dh) for h in range(heads)]
        t = [dict() for _ in range(heads)]

        def operands(h):
            e = t[h]
            e["kb"] = (qk[r0:r0 + CHUNK, dm + h * dh:dm + (h + 1) * dh] * (dh ** -0.5)).astype(BF16)
            e["qt"] = qk[r0:r0 + CHUNK, hsl[h]].astype(BF16).T
            vext = jnp.concatenate([v_ref[r0:r0 + CHUNK, hsl[h]], ones_col], axis=1)
            e["vt"] = vext.T[0:ext, :]
            e["kq"] = _dot(e["kb"], e["qt"])
            e["ct"] = ct_state[h]
            e["cq"] = _dot(e["ct"].astype(BF16), e["qt"])

        def decay_matrix(h):
            e = t[h]
            e["brow"], e["irow"] = pt[heads + h:heads + h + 1, :], pt[h:h + 1, :]
            cvec = p[:, h:h + 1] - p[:, heads + h:heads + h + 1]
            e["m_prev"] = m_state[h:h + 1, 0:1]
            dt = jnp.where(upper, cvec + e["brow"], -jnp.inf)
            inter = e["brow"] + e["m_prev"]
            e["m_t"] = jnp.maximum(inter, jnp.max(dt, axis=0, keepdims=True))
            e["inter_w"] = jnp.exp(inter - e["m_t"])
            st = e.pop("kq") * jnp.exp(dt - e["m_t"])
            e["sv"] = _dot(e["vt"], st.astype(BF16))

        def state_update(h):
            e = t[h]
            b_last = e["brow"][:, CHUNK - 1:CHUNK]
            a_end = b_last - e["brow"] + e["irow"]
            m_new = jnp.maximum(b_last + e["m_prev"], jnp.max(a_end, axis=1, keepdims=True))
            vw = (e["vt"] * jnp.exp(a_end - m_new)).astype(BF16)
            decay = jnp.exp(b_last + e["m_prev"] - m_new)
            ct_state[h] = decay * e["ct"] + _dot(vw, e["kb"])
            m_state[h:h + 1, :] = jnp.broadcast_to(m_new, (1, LANES))

        def output(h):
            e = t[h]
            ne = e["sv"] + e["inter_w"] * e["cq"]
            den = ne[dh:dh + 1, :]
            ht = ne[0:dh, :] / jnp.maximum(jnp.abs(den), jnp.exp(-e["m_t"]))
            hn = _rms(ht, axis=0).T * gh_ref[:, hsl[h]]
            y_ref[r0:r0 + CHUNK, hsl[h]] = (_sigmoid(o_ref[r0:r0 + CHUNK, hsl[h]]) * hn
                                            * _silu(z_ref[r0:r0 + CHUNK, hsl[h]]))
            e.clear()

        stages = (operands, decay_matrix, state_update, output)
        for step in range(heads + len(stages) - 1):
            for lag, stage in enumerate(stages):
                if 0 <= step - lag < heads:
                    stage(step - lag)


def _sb_kernel(q_ref, k_ref, v_ref, z_ref, g_ref, y_ref, acc_ref, carry_ref, pend_ref,
               *, tq, tk, heads, dh):
    i = pl.program_id(1)
    nsub = tq // tk
    jj = lax.broadcasted_iota(jnp.int32, (SB_CUMSUM_TERMS * tk, tk), 0) % tk
    ss = lax.broadcasted_iota(jnp.int32, (SB_CUMSUM_TERMS * tk, tk), 1)
    uu = jnp.where(jj >= ss, 1.0, 0.0).astype(BF16)
    strict = (lax.broadcasted_iota(jnp.int32, (tk, tk), 1)
              < lax.broadcasted_iota(jnp.int32, (tk, tk), 0))
    hs = [slice(h * dh, (h + 1) * dh) for h in range(heads)]
    strips = range(0, tk, SB_STRIP)

    def sweep(base, pairs, fresh, pending, drain):
        items = [(sub, koff, masked, h) for sub, koff, masked in pairs for h in range(heads)]
        z, hl, c = {}, {}, {}
        if pending:
            items = items + [(nsub - 1, 1, False, heads - 2), (nsub - 1, 1, False, heads - 1)]
            z[-2], c[-2], z[-1] = pend_ref[0], pend_ref[1], pend_ref[2]

        def kv_tile(ref, koff, h):
            k0 = (base + koff) * tk
            k0 = k0 if isinstance(k0, int) else pl.multiple_of(k0, tk)
            return ref[pl.ds(k0, tk), hs[h]]

        def scores(n):
            sub, koff, _, h = items[n]
            z[n] = _dot_nt(q_ref[sub * tk:(sub + 1) * tk, hs[h]], kv_tile(k_ref, koff, h))

        def softplus_split(n):
            parts = []
            for r in strips:
                zs = z[n][r:r + SB_STRIP]
                sp = jnp.maximum(zs, 0.0) + jnp.log(1.0 + jnp.exp2(jnp.abs(zs) * -LOG2E))
                if items[n][2]:
                    sp = jnp.where(strict[r:r + SB_STRIP], sp, 0.0)
                parts.append(jnp.concatenate(_split_bf16(sp, SB_CUMSUM_TERMS), axis=1))
            hl[n] = jnp.concatenate(parts, axis=0)

        def cumsum(n):
            c[n] = _dot(hl.pop(n), uu)

        def weights_times_v(n):
            sub, koff, masked, h = items[n]
            zn, cn = z.pop(n), c.pop(n)
            parts = []
            for r in strips:
                a = jnp.exp2((zn[r:r + SB_STRIP] - cn[r:r + SB_STRIP]) * LOG2E)
                if masked:
                    a = jnp.where(strict[r:r + SB_STRIP], a, 0.0)
                parts.append(a.astype(BF16))
            av = _dot(jnp.concatenate(parts, axis=0), kv_tile(v_ref, koff, h))
            slot = sub * heads + h
            if (sub, h) in fresh:
                fresh.discard((sub, h))
                carry_ref[slot] = cn[:, 0:1]
                acc_ref[slot] = av
            else:
                carry = carry_ref[slot]
                carry_ref[slot] = carry + cn[:, 0:1]
                acc_ref[slot] = acc_ref[slot] + jnp.exp2(carry * -LOG2E) * av

        n_new = len(items) - (2 if pending else 0)
        first = -2 if pending else 0
        for step in range(n_new + (2 if drain else 0)):
            if step < n_new:
                scores(step)
            if max(first, -1) <= step - 1 < n_new:
                softplus_split(step - 1)
                cumsum(step - 1)
            if first <= step - 2 < n_new:
                weights_times_v(step - 2)
        if not drain:
            pend_ref[0], pend_ref[1], pend_ref[2] = z[n_new - 2], c[n_new - 2], z[n_new - 1]

    diag = [(sub, koff, koff == sub) for sub in range(nsub) for koff in range(sub, -1, -1)]
    full = [(sub, 0, False) for sub in range(nsub)]
    fresh = {(sub, h) for sub in range(nsub) for h in range(heads)}
    sweep(i * nsub, diag, fresh, False, False)
    for sub, h in sorted(fresh):
        carry_ref[sub * heads + h] = jnp.zeros((tk, 1), F32)
        acc_ref[sub * heads + h] = jnp.zeros((tk, dh), F32)

    def body(n, _):
        sweep(i * nsub - 1 - n, full, set(), True, False)
        return 0

    lax.fori_loop(0, i * nsub, body, 0)
    sweep(-1, [], set(), True, True)

    for sub in range(nsub):
        hn = jnp.concatenate([_rms(acc_ref[sub * heads + h]) for h in range(heads)], axis=1)
        rows = slice(sub * tk, (sub + 1) * tk)
        y_ref[rows, :] = hn * g_ref[...] * _silu(z_ref[rows, :])


def _out_kernel(x_ref, ym_ref, ys_ref, qc_ref, zc_ref, k_ref, vt_ref, gmh_ref, wo_ref, gp_ref,
                out_ref, *, heads, dh, dm, ds):
    qt = qc_ref[...].T
    hs = [slice(h * dh, (h + 1) * dh) for h in range(heads)]
    sc, pr, den, o = {}, {}, {}, {}

    def scores(h):
        sc[h] = _dot(k_ref[0, :, hs[h]], qt[hs[h], :])

    def softmax(h):
        pr[h] = jnp.exp(sc[h] - jnp.max(sc[h], axis=0, keepdims=True))
        den[h] = jnp.sum(pr[h], axis=0, keepdims=True)
        o[h] = _dot(vt_ref[0, hs[h], :], pr[h].astype(BF16))

    def norm(h):
        o[h] = _rms(o[h] / den[h], axis=0)

    for step in range(heads + 2):
        for stage, lag in ((scores, 0), (softmax, 1), (norm, 2)):
            if 0 <= step - lag < heads:
                stage(step - lag)
    y = _dot(ym_ref[...].astype(BF16), wo_ref[0:dm, :]) + _dot(ys_ref[...].astype(BF16), wo_ref[dm:dm + ds, :])
    yc = jnp.concatenate([o[h] for h in range(heads)], axis=0).T * gmh_ref[...] * _silu(zc_ref[...])
    ycb = yc.astype(BF16)
    half = x_ref.shape[0] // 2
    for r in (0, half):
        yr = y[r:r + half] + _dot(ycb[r:r + half], wo_ref[dm + ds:, :])
        out_ref[r:r + half, :] = x_ref[r:r + half, :] + _rms(yr) * gp_ref[...]


def _params(*sem):
    return pltpu.CompilerParams(dimension_semantics=sem, vmem_limit_bytes=VMEM_LIMIT)


def _col(width, offset):
    assert offset % width == 0
    return offset // width


def kernel(x, mem, g_pre, w_in, conv_w, conv_b, i_bias, f_bias, g_mlstm_head, g_sb_head,
           g_mem_head, g_mem, w_mem_kv, w_out, g_post):
    B, S, D = x.shape
    M = mem.shape[1]
    dm, ds, dc = g_mlstm_head.shape[0], g_sb_head.shape[0], g_mem_head.shape[0]
    hm, hsb, hc = MLSTM_HEADS, SB_HEADS, MEM_HEADS
    ts = ROW_TILE
    assert S % ts == 0 and ts % CHUNK == 0 and 2 * hm <= LANES
    nt = S // ts

    cols, off = {}, 0
    for name, width in (("q_m", dm), ("k_m", dm), ("v_m", dm), ("o_m", dm), ("z_m", dm),
                        ("if", 2 * hm), ("q_s", ds), ("k_s", ds), ("v_s", ds), ("z_s", ds),
                        ("q_c", dc), ("z_c", dc)):
        cols[name] = w_in[:, off:off + width]
        off += width
    sb_scale, mem_scale = (ds // hsb) ** -0.5, (dc // hc) ** -0.5
    assert math.log2(sb_scale).is_integer() and math.log2(mem_scale).is_integer()
    w32 = jnp.concatenate([cols[n] for n in ("q_m", "k_m", "o_m", "z_m", "z_s", "z_c")], axis=1).astype(BF16)
    w16 = jnp.concatenate([cols["v_m"], cols["q_s"] * sb_scale, cols["k_s"], cols["v_s"],
                           cols["q_c"] * mem_scale], axis=1).astype(BF16)
    w_gate = jnp.pad(cols["if"], ((0, 0), (0, LANES - 2 * hm))).astype(BF16)
    gate_bias = jnp.pad(jnp.concatenate([i_bias, f_bias]), (0, LANES - 2 * hm)).reshape(1, LANES)
    n32, n16 = w32.shape[1], w16.shape[1]
    o32 = {"qk_m": 0, "o_m": 2 * dm, "z_m": 3 * dm, "z_s": 4 * dm, "z_c": 4 * dm + ds}
    o16 = {"v_m": 0, "q_s": dm, "k_s": dm + ds, "v_s": dm + 2 * ds, "q_c": dm + 3 * ds}
    row2 = lambda a: a.reshape(1, -1)

    x2 = x.reshape(B * S, D)
    const = lambda *_: (0, 0)

    def rows(width, offset):
        c = _col(width, offset)
        return pl.BlockSpec((ts, width), lambda b, t: (b * nt + t, c))

    tp = PROJ_TILE
    assert (B * S) % tp == 0
    prow = lambda width: pl.BlockSpec((tp, width), lambda r: (r, 0))
    u32, u16, gates = pl.pallas_call(
        _inproj_kernel,
        grid=(B * S // tp,),
        in_specs=[prow(D),
                  pl.BlockSpec((1, D), const),
                  pl.BlockSpec((D, n32), const),
                  pl.BlockSpec((D, n16), const),
                  pl.BlockSpec((D, LANES), const)],
        out_specs=[prow(n32), prow(n16), prow(LANES)],
        out_shape=[jax.ShapeDtypeStruct((B * S, n32), F32),
                   jax.ShapeDtypeStruct((B * S, n16), BF16),
                   jax.ShapeDtypeStruct((B * S, LANES), F32)],
        compiler_params=_params("arbitrary"),
        name="in_proj",
    )(x2, row2(g_pre), w32, w16, w_gate)

    k_c, vt_c = pl.pallas_call(
        _memkv_kernel,
        grid=(B,),
        in_specs=[pl.BlockSpec((1, M, D), lambda b: (b, 0, 0)),
                  pl.BlockSpec((1, D), const),
                  pl.BlockSpec((D, 2 * dc), const)],
        out_specs=[pl.BlockSpec((1, M, dc), lambda b: (b, 0, 0)),
                   pl.BlockSpec((1, dc, M), lambda b: (b, 0, 0))],
        out_shape=[jax.ShapeDtypeStruct((B, M, dc), BF16),
                   jax.ShapeDtypeStruct((B, dc, M), BF16)],
        compiler_params=_params("arbitrary"),
        name="mem_kv",
    )(mem, row2(g_mem), w_mem_kv.astype(BF16))

    y_m = pl.pallas_call(
        functools.partial(_mlstm_kernel, ts=ts, heads=hm, dh=dm // hm),
        grid=(B, nt),
        in_specs=[rows(2 * dm, o32["qk_m"]), rows(dm, o16["v_m"]), rows(dm, o32["o_m"]),
                  rows(dm, o32["z_m"]), rows(LANES, 0),
                  pl.BlockSpec((CONV_WIDTH, 2 * dm), const),
                  pl.BlockSpec((1, 2 * dm), const),
                  pl.BlockSpec((1, LANES), const),
                  pl.BlockSpec((1, dm), const)],
        out_specs=rows(dm, 0),
        out_shape=jax.ShapeDtypeStruct((B * S, dm), F32),
        scratch_shapes=[pltpu.VMEM((ts + SUBLANES, 2 * dm), F32),
                        pltpu.VMEM((hm, dm // hm + 2 * SUBLANES, dm // hm), F32),
                        pltpu.VMEM((SUBLANES, LANES), F32)],
        compiler_params=_params("arbitrary", "arbitrary"),
        name="mlstm",
    )(u32, u16, u32, u32, gates, conv_w, row2(conv_b), gate_bias, row2(g_mlstm_head))

    tq = SB_QTILE
    assert S % tq == 0 and tq % SB_KTILE == 0 and SB_KTILE % SB_STRIP == 0
    ntq = S // tq

    def qrows(width, offset):
        c = _col(width, offset)
        return pl.BlockSpec((tq, width), lambda b, t: (b * ntq + t, c))

    y_s = pl.pallas_call(
        functools.partial(_sb_kernel, tq=tq, tk=SB_KTILE, heads=hsb, dh=ds // hsb),
        grid=(B, ntq),
        in_specs=[qrows(ds, o16["q_s"]),
                  pl.BlockSpec((S, ds), lambda b, t: (b, _col(ds, o16["k_s"]))),
                  pl.BlockSpec((S, ds), lambda b, t: (b, _col(ds, o16["v_s"]))),
                  qrows(ds, o32["z_s"]),
                  pl.BlockSpec((1, ds), const)],
        out_specs=qrows(ds, 0),
        out_shape=jax.ShapeDtypeStruct((B * S, ds), F32),
        scratch_shapes=[pltpu.VMEM((tq // SB_KTILE * hsb, SB_KTILE, ds // hsb), F32),
                        pltpu.VMEM((tq // SB_KTILE * hsb, SB_KTILE, 1), F32),
                        pltpu.VMEM((3, SB_KTILE, SB_KTILE), F32)],
        compiler_params=_params("arbitrary", "arbitrary"),
        name="stick_breaking",
    )(u16, u16, u16, u32, row2(g_sb_head))

    out = pl.pallas_call(
        functools.partial(_out_kernel, heads=hc, dh=dc // hc, dm=dm, ds=ds),
        grid=(B, nt),
        in_specs=[rows(D, 0), rows(dm, 0), rows(ds, 0),
                  rows(dc, o16["q_c"]), rows(dc, o32["z_c"]),
                  pl.BlockSpec((1, M, dc), lambda b, t: (b, 0, 0)),
                  pl.BlockSpec((1, dc, M), lambda b, t: (b, 0, 0)),
                  pl.BlockSpec((1, dc), const),
                  pl.BlockSpec((dm + ds + dc, D), const),
                  pl.BlockSpec((1, D), const)],
        out_specs=rows(D, 0),
        out_shape=jax.ShapeDtypeStruct((B * S, D), x.dtype),
        compiler_params=_params("arbitrary", "arbitrary"),
        name="mem_attn_out_proj",
    )(x2, y_m, y_s, u16, u32, k_c, vt_c, row2(g_mem_head), w_out.astype(BF16), row2(g_post))

    return out.reshape(B, S, D)
```

```python
import functools
import math

import jax
import jax.numpy as jnp
from jax import lax
from jax.experimental import pallas as pl
from jax.experimental.pallas import tpu as pltpu

F32 = jnp.float32
BF16 = jnp.bfloat16
EPS = 1e-6
LANES = 128
SUBLANES = 8
VMEM_LIMIT = 56 * 2 ** 20

MLSTM_HEADS = 4
SB_HEADS = 4
MEM_HEADS = 4
CONV_WIDTH = 4
CHUNK = 128
ROW_TILE = 1024
PROJ_TILE = 512
PROJ_ROWS = 256
PROJ_COLS = 512
OUT_TAIL_BLOCKS = 2
SB_QTILE = 1024
SB_KTILE = 256
SB_STRIP = 32
SB_CUMSUM_TERMS = 1
LOG2E = 1.4426950408889634


def _sigmoid(x):
    return 1.0 / (1.0 + jnp.exp2(x * -LOG2E))


def _silu(x):
    return x * _sigmoid(x)


def _softplus(x):
    return jnp.maximum(x, 0.0) + jnp.log(1.0 + jnp.exp(-jnp.abs(x)))


def _dot(a, b):
    return jnp.dot(a, b, preferred_element_type=F32)


def _dot_nt(a, b):
    return lax.dot_general(a, b, (((1,), (1,)), ((), ())), preferred_element_type=F32)


def _split_bf16(x, parts):
    out, rem = [], x
    for n in range(parts):
        p = rem.astype(BF16)
        out.append(p)
        if n + 1 < parts:
            rem = rem - p.astype(F32)
    return out


def _rms(x, axis=-1):
    return x * lax.rsqrt(jnp.mean(x * x, axis=axis, keepdims=True) + EPS)


def _proj_mlstm_kernel(x_ref, gpre_ref, w_ref, cw_ref, cb_ref, gb_ref, gh_ref,
                       u16_ref, zz_ref, y_ref, qk_s, oz_s, v_s, gate_s, ct_state, m_state,
                       *, ts, nt, heads, dh, groups):
    g = pl.program_id(0)
    dm = heads * dh
    pad = SUBLANES
    ext = ct_state.shape[1]
    wslot = lax.rem(g, 2)
    rslot = 1 - wslot
    first = jnp.logical_or(g == 0, lax.rem(g - 1, nt) == 0)

    @pl.when(g == 0)
    def _():
        qk_s[...] = jnp.zeros_like(qk_s)
        oz_s[...] = jnp.zeros_like(oz_s)
        v_s[...] = jnp.zeros_like(v_s)
        gate_s[...] = jnp.zeros_like(gate_s)

    @pl.when(first)
    def _():
        ct_state[...] = jnp.zeros_like(ct_state)
        m_state[...] = jnp.zeros_like(m_state)

    tail = qk_s[wslot, ts:ts + pad, :]
    qk_s[rslot, 0:pad, :] = jnp.where(first, 0.0, tail)

    hb = {}
    pieces = [(r, name, c) for r in range(0, ts, PROJ_ROWS) for name in ("qk", "oz", "zz", "v", "u16", "gate")
              for c in range(0, groups[name][1], PROJ_COLS)]

    def project_piece():
        if not pieces:
            return
        r, name, c = pieces.pop(0)
        if r not in hb:
            hb[r] = (_rms(x_ref[r:r + PROJ_ROWS, :]) * gpre_ref[...]).astype(BF16)
        start, width = groups[name]
        w = min(PROJ_COLS, width - c)
        res = _dot(hb[r], w_ref[:, start + c:start + c + w])
        rows, cs = slice(r, r + PROJ_ROWS), slice(c, c + w)
        if name == "qk":
            qk_s[wslot, pad + r:pad + r + PROJ_ROWS, cs] = res
        elif name == "oz":
            oz_s[wslot, rows, cs] = res
        elif name == "v":
            v_s[wslot, rows, cs] = res.astype(BF16)
        elif name == "gate":
            gate_s[wslot, rows, cs] = res
        elif name == "u16":
            u16_ref[rows, cs] = res.astype(BF16)
        else:
            zz_ref[rows, cs] = res

    lane = lax.broadcasted_iota(jnp.int32, (CHUNK, LANES), 1)
    row = lax.broadcasted_iota(jnp.int32, (CHUNK, CHUNK), 0)
    col = lax.broadcasted_iota(jnp.int32, (CHUNK, CHUNK), 1)
    tril_bf = jnp.where(row >= col, 1.0, 0.0).astype(BF16)
    upper = row <= col
    ones_col = jnp.where(lane == 0, 1.0, 0.0).astype(BF16)
    hsl = [slice(h * dh, (h + 1) * dh) for h in range(heads)]

    for c in range(ts // CHUNK):
        r0 = c * CHUNK
        acc = None
        for j in range(CONV_WIDTH):
            off = pad - (CONV_WIDTH - 1) + j + r0
            term = qk_s[rslot, off:off + CHUNK, :] * cw_ref[j:j + 1, :]
            acc = term if acc is None else acc + term
        qk = _silu(acc + cb_ref[...])
        project_piece()

        gt = gate_s[rslot, r0:r0 + CHUNK, :] + gb_ref[...]
        logf = -_softplus(-gt)
        cum = _dot(tril_bf, jnp.concatenate(_split_bf16(logf, 3), axis=1))
        bcum = cum[:, :LANES] + cum[:, LANES:2 * LANES] + cum[:, 2 * LANES:]
        p = jnp.where(lane < heads, gt, bcum)
        pt = p.T
        t = [dict() for _ in range(heads)]

        def operands(h):
            e = t[h]
            e["kb"] = (qk[:, dm + h * dh:dm + (h + 1) * dh] * (dh ** -0.5)).astype(BF16)
            e["qt"] = qk[:, hsl[h]].astype(BF16).T
            vext = jnp.concatenate([v_s[rslot, r0:r0 + CHUNK, hsl[h]], ones_col], axis=1)
            e["vt"] = vext.T[0:ext, :]
            e["kq"] = _dot(e["kb"], e["qt"])
            e["ct"] = ct_state[h]
            e["cq"] = _dot(e["ct"].astype(BF16), e["qt"])

        def decay_matrix(h):
            e = t[h]
            e["brow"], e["irow"] = pt[heads + h:heads + h + 1, :], pt[h:h + 1, :]
            cvec = p[:, h:h + 1] - p[:, heads + h:heads + h + 1]
            e["m_prev"] = m_state[h:h + 1, 0:1]
            dt = jnp.where(upper, cvec + e["brow"], -jnp.inf)
            inter = e["brow"] + e["m_prev"]
            e["m_t"] = jnp.maximum(inter, jnp.max(dt, axis=0, keepdims=True))
            e["inter_w"] = jnp.exp(inter - e["m_t"])
            st = e.pop("kq") * jnp.exp(dt - e["m_t"])
            e["sv"] = _dot(e["vt"], st.astype(BF16))

        def state_update(h):
            e = t[h]
            b_last = e["brow"][:, CHUNK - 1:CHUNK]
            a_end = b_last - e["brow"] + e["irow"]
            m_new = jnp.maximum(b_last + e["m_prev"], jnp.max(a_end, axis=1, keepdims=True))
            vw = (e["vt"] * jnp.exp(a_end - m_new)).astype(BF16)
            decay = jnp.exp(b_last + e["m_prev"] - m_new)
            ct_state[h] = decay * e["ct"] + _dot(vw, e["kb"])
            m_state[h:h + 1, :] = jnp.broadcast_to(m_new, (1, LANES))

        def output(h):
            e = t[h]
            ne = e["sv"] + e["inter_w"] * e["cq"]
            den = ne[dh:dh + 1, :]
            ht = ne[0:dh, :] / jnp.maximum(jnp.abs(den), jnp.exp(-e["m_t"]))
            hn = _rms(ht, axis=0).T * gh_ref[:, hsl[h]]
            y_ref[r0:r0 + CHUNK, hsl[h]] = (_sigmoid(oz_s[rslot, r0:r0 + CHUNK, hsl[h]]) * hn
                                            * _silu(oz_s[rslot, r0:r0 + CHUNK, dm + h * dh:dm + (h + 1) * dh]))
            e.clear()

        stages = (operands, decay_matrix, state_update, output)
        for step in range(heads + len(stages) - 1):
            for lag, stage in enumerate(stages):
                if 0 <= step - lag < heads:
                    stage(step - lag)
            if step < heads:
                project_piece()

    while pieces:
        project_piece()


def _memkv_kernel(mem_ref, g_ref, w_ref, k_ref, vt_ref):
    h = (_rms(mem_ref[0]) * g_ref[...]).astype(BF16)
    kv = _dot(h, w_ref[...])
    dc = kv.shape[1] // 2
    k_ref[0] = kv[:, :dc].astype(BF16)
    vt_ref[0] = kv[:, dc:].T.astype(BF16)


def _sb_kernel(q_ref, k_ref, v_ref, z_ref, g_ref, y_ref, acc_ref, carry_ref, pend_ref,
               *, tq, tk, heads, dh):
    i = pl.program_id(1)
    nsub = tq // tk
    jj = lax.broadcasted_iota(jnp.int32, (SB_CUMSUM_TERMS * tk, tk), 0) % tk
    ss = lax.broadcasted_iota(jnp.int32, (SB_CUMSUM_TERMS * tk, tk), 1)
    uu = jnp.where(jj >= ss, 1.0, 0.0).astype(BF16)
    strict = (lax.broadcasted_iota(jnp.int32, (tk, tk), 1)
              < lax.broadcasted_iota(jnp.int32, (tk, tk), 0))
    hs = [slice(h * dh, (h + 1) * dh) for h in range(heads)]
    strips = range(0, tk, SB_STRIP)

    def sweep(base, pairs, fresh, pending, drain):
        items = [(sub, koff, masked, h) for sub, koff, masked in pairs for h in range(heads)]
        z, hl, c = {}, {}, {}
        if pending:
            items = items + [(nsub - 1, 1, False, heads - 2), (nsub - 1, 1, False, heads - 1)]
            z[-2], c[-2], z[-1] = pend_ref[0], pend_ref[1], pend_ref[2]

        def kv_tile(ref, koff, h):
            k0 = (base + koff) * tk
            k0 = k0 if isinstance(k0, int) else pl.multiple_of(k0, tk)
            return ref[pl.ds(k0, tk), hs[h]]

        def scores(n):
            sub, koff, _, h = items[n]
            z[n] = _dot_nt(q_ref[sub * tk:(sub + 1) * tk, hs[h]], kv_tile(k_ref, koff, h))

        def softplus_split(n):
            parts = []
            for r in strips:
                zs = z[n][r:r + SB_STRIP]
                sp = jnp.maximum(zs, 0.0) + jnp.log(1.0 + jnp.exp2(jnp.abs(zs) * -LOG2E))
                if items[n][2]:
                    sp = jnp.where(strict[r:r + SB_STRIP], sp, 0.0)
                parts.append(jnp.concatenate(_split_bf16(sp, SB_CUMSUM_TERMS), axis=1))
            hl[n] = jnp.concatenate(parts, axis=0)

        def cumsum(n):
            c[n] = _dot(hl.pop(n), uu)

        def weights_times_v(n):
            sub, koff, masked, h = items[n]
            zn, cn = z.pop(n), c.pop(n)
            parts = []
            for r in strips:
                a = jnp.exp2((zn[r:r + SB_STRIP] - cn[r:r + SB_STRIP]) * LOG2E)
                if masked:
                    a = jnp.where(strict[r:r + SB_STRIP], a, 0.0)
                parts.append(a.astype(BF16))
            av = _dot(jnp.concatenate(parts, axis=0), kv_tile(v_ref, koff, h))
            slot = sub * heads + h
            if (sub, h) in fresh:
                fresh.discard((sub, h))
                carry_ref[slot] = cn[:, 0:1]
                acc_ref[slot] = av
            else:
                carry = carry_ref[slot]
                carry_ref[slot] = carry + cn[:, 0:1]
                acc_ref[slot] = acc_ref[slot] + jnp.exp2(carry * -LOG2E) * av

        n_new = len(items) - (2 if pending else 0)
        first = -2 if pending else 0
        for step in range(n_new + (2 if drain else 0)):
            if step < n_new:
                scores(step)
            if max(first, -1) <= step - 1 < n_new:
                softplus_split(step - 1)
                cumsum(step - 1)
            if first <= step - 2 < n_new:
                weights_times_v(step - 2)
        if not drain:
            pend_ref[0], pend_ref[1], pend_ref[2] = z[n_new - 2], c[n_new - 2], z[n_new - 1]

    diag = [(sub, koff, koff == sub) for sub in range(nsub) for koff in range(sub, -1, -1)]
    full = [(sub, 0, False) for sub in range(nsub)]
    fresh = {(sub, h) for sub in range(nsub) for h in range(heads)}
    sweep(i * nsub, diag, fresh, False, False)
    for sub, h in sorted(fresh):
        carry_ref[sub * heads + h] = jnp.zeros((tk, 1), F32)
        acc_ref[sub * heads + h] = jnp.zeros((tk, dh), F32)

    def body(n, _):
        sweep(i * nsub - 1 - n, full, set(), True, False)
        return 0

    lax.fori_loop(0, i * nsub, body, 0)
    sweep(-1, [], set(), True, True)

    for sub in range(nsub):
        hn = jnp.concatenate([_rms(acc_ref[sub * heads + h]) for h in range(heads)], axis=1)
        rows = slice(sub * tk, (sub + 1) * tk)
        y_ref[rows, :] = hn * g_ref[...] * _silu(z_ref[rows, :])


def _out_kernel(x_ref, ym_ref, ys_ref, qc_ref, zc_ref, k_ref, vt_ref, gmh_ref, wo_ref, gp_ref,
                out_ref, *, heads, dh, dm, ds):
    qt = qc_ref[...].T
    hs = [slice(h * dh, (h + 1) * dh) for h in range(heads)]
    sc, pr, den, o = {}, {}, {}, {}

    def scores(h):
        sc[h] = _dot(k_ref[0, :, hs[h]], qt[hs[h], :])

    def softmax(h):
        pr[h] = jnp.exp(sc[h] - jnp.max(sc[h], axis=0, keepdims=True))
        den[h] = jnp.sum(pr[h], axis=0, keepdims=True)
        o[h] = _dot(vt_ref[0, hs[h], :], pr[h].astype(BF16))

    def norm(h):
        o[h] = _rms(o[h] / den[h], axis=0)

    ymb, ysb = ym_ref[...].astype(BF16), ys_ref[...].astype(BF16)
    width = out_ref.shape[1] // heads

    def project(n):
        cs = slice(n * width, (n + 1) * width)
        out_ref[:, cs] = _dot(ymb, wo_ref[0:dm, cs]) + _dot(ysb, wo_ref[dm:dm + ds, cs])

    for step in range(heads + 2):
        for stage, lag in ((scores, 0), (softmax, 1), (norm, 2)):
            if 0 <= step - lag < heads:
                stage(step - lag)
        if 0 <= step - 1 < heads:
            project(step - 1)
    yc = jnp.concatenate([o[h] for h in range(heads)], axis=0).T * gmh_ref[...] * _silu(zc_ref[...])
    ycb = yc.astype(BF16)
    blk = x_ref.shape[0] // OUT_TAIL_BLOCKS
    for r in range(0, x_ref.shape[0], blk):
        yr = out_ref[r:r + blk, :] + _dot(ycb[r:r + blk], wo_ref[dm + ds:, :])
        out_ref[r:r + blk, :] = x_ref[r:r + blk, :] + _rms(yr) * gp_ref[...]


def _params(*sem):
    return pltpu.CompilerParams(dimension_semantics=sem, vmem_limit_bytes=VMEM_LIMIT)


def _col(width, offset):
    assert offset % width == 0
    return offset // width


def kernel(x, mem, g_pre, w_in, conv_w, conv_b, i_bias, f_bias, g_mlstm_head, g_sb_head,
           g_mem_head, g_mem, w_mem_kv, w_out, g_post):
    B, S, D = x.shape
    M = mem.shape[1]
    dm, ds, dc = g_mlstm_head.shape[0], g_sb_head.shape[0], g_mem_head.shape[0]
    hm, hsb, hc = MLSTM_HEADS, SB_HEADS, MEM_HEADS
    ts = ROW_TILE
    assert S % ts == 0 and ts % CHUNK == 0 and 2 * hm <= LANES
    nt = S // ts

    cols, off = {}, 0
    for name, width in (("q_m", dm), ("k_m", dm), ("v_m", dm), ("o_m", dm), ("z_m", dm),
                        ("if", 2 * hm), ("q_s", ds), ("k_s", ds), ("v_s", ds), ("z_s", ds),
                        ("q_c", dc), ("z_c", dc)):
        cols[name] = w_in[:, off:off + width]
        off += width
    sb_scale, mem_scale = (ds // hsb) ** -0.5, (dc // hc) ** -0.5
    assert math.log2(sb_scale).is_integer() and math.log2(mem_scale).is_integer()
    parts = (("qk", [cols["q_m"], cols["k_m"]]), ("oz", [cols["o_m"], cols["z_m"]]),
             ("zz", [cols["z_s"], cols["z_c"]]), ("v", [cols["v_m"]]),
             ("u16", [cols["q_s"] * sb_scale, cols["k_s"], cols["v_s"], cols["q_c"] * mem_scale]),
             ("gate", [jnp.pad(cols["if"], ((0, 0), (0, LANES - 2 * hm)))]))
    groups, start = {}, 0
    for name, arrs in parts:
        width = sum(a.shape[1] for a in arrs)
        groups[name] = (start, width)
        start += width
    w_cat = jnp.concatenate([a for _, arrs in parts for a in arrs], axis=1).astype(BF16)
    gate_bias = jnp.pad(jnp.concatenate([i_bias, f_bias]), (0, LANES - 2 * hm)).reshape(1, LANES)
    o16 = {"q_s": 0, "k_s": ds, "v_s": 2 * ds, "q_c": 3 * ds}
    ozz = {"z_s": 0, "z_c": ds}
    row2 = lambda a: a.reshape(1, -1)

    x2 = x.reshape(B * S, D)
    const = lambda *_: (0, 0)

    def rows(width, offset):
        c = _col(width, offset)
        return pl.BlockSpec((ts, width), lambda b, t: (b * nt + t, c))

    tp = PROJ_TILE
    assert S % tp == 0 and tp % CHUNK == 0 and tp % PROJ_ROWS == 0
    n_tiles = B * S // tp
    cur = lambda width: pl.BlockSpec((tp, width), lambda g: (jnp.minimum(g, n_tiles - 1), 0))
    u16, zz, y_m = pl.pallas_call(
        functools.partial(_proj_mlstm_kernel, ts=tp, nt=S // tp, heads=hm, dh=dm // hm, groups=groups),
        grid=(n_tiles + 1,),
        in_specs=[cur(D),
                  pl.BlockSpec((1, D), const),
                  pl.BlockSpec((D, w_cat.shape[1]), const),
                  pl.BlockSpec((CONV_WIDTH, 2 * dm), const),
                  pl.BlockSpec((1, 2 * dm), const),
                  pl.BlockSpec((1, LANES), const),
                  pl.BlockSpec((1, dm), const)],
        out_specs=[cur(groups["u16"][1]), cur(groups["zz"][1]),
                   pl.BlockSpec((tp, dm), lambda g: (jnp.maximum(g - 1, 0), 0))],
        out_shape=[jax.ShapeDtypeStruct((B * S, groups["u16"][1]), BF16),
                   jax.ShapeDtypeStruct((B * S, groups["zz"][1]), F32),
                   jax.ShapeDtypeStruct((B * S, dm), F32)],
        scratch_shapes=[pltpu.VMEM((2, tp + SUBLANES, 2 * dm), F32),
                        pltpu.VMEM((2, tp, 2 * dm), F32),
                        pltpu.VMEM((2, tp, dm), BF16),
                        pltpu.VMEM((2, tp, LANES), F32),
                        pltpu.VMEM((hm, dm // hm + 2 * SUBLANES, dm // hm), F32),
                        pltpu.VMEM((SUBLANES, LANES), F32)],
        compiler_params=_params("arbitrary"),
        name="proj_mlstm",
    )(x2, row2(g_pre), w_cat, conv_w, row2(conv_b), gate_bias, row2(g_mlstm_head))

    k_c, vt_c = pl.pallas_call(
        _memkv_kernel,
        grid=(B,),
        in_specs=[pl.BlockSpec((1, M, D), lambda b: (b, 0, 0)),
                  pl.BlockSpec((1, D), const),
                  pl.BlockSpec((D, 2 * dc), const)],
        out_specs=[pl.BlockSpec((1, M, dc), lambda b: (b, 0, 0)),
                   pl.BlockSpec((1, dc, M), lambda b: (b, 0, 0))],
        out_shape=[jax.ShapeDtypeStruct((B, M, dc), BF16),
                   jax.ShapeDtypeStruct((B, dc, M), BF16)],
        compiler_params=_params("arbitrary"),
        name="mem_kv",
    )(mem, row2(g_mem), w_mem_kv.astype(BF16))

    tq = SB_QTILE
    assert S % tq == 0 and tq % SB_KTILE == 0 and SB_KTILE % SB_STRIP == 0
    ntq = S // tq

    def qrows(width, offset):
        c = _col(width, offset)
        return pl.BlockSpec((tq, width), lambda b, t: (b * ntq + t, c))

    y_s = pl.pallas_call(
        functools.partial(_sb_kernel, tq=tq, tk=SB_KTILE, heads=hsb, dh=ds // hsb),
        grid=(B, ntq),
        in_specs=[qrows(ds, o16["q_s"]),
                  pl.BlockSpec((S, ds), lambda b, t: (b, _col(ds, o16["k_s"]))),
                  pl.BlockSpec((S, ds), lambda b, t: (b, _col(ds, o16["v_s"]))),
                  qrows(ds, ozz["z_s"]),
                  pl.BlockSpec((1, ds), const)],
        out_specs=qrows(ds, 0),
        out_shape=jax.ShapeDtypeStruct((B * S, ds), F32),
        scratch_shapes=[pltpu.VMEM((tq // SB_KTILE * hsb, SB_KTILE, ds // hsb), F32),
                        pltpu.VMEM((tq // SB_KTILE * hsb, SB_KTILE, 1), F32),
                        pltpu.VMEM((3, SB_KTILE, SB_KTILE), F32)],
        compiler_params=_params("arbitrary", "arbitrary"),
        name="stick_breaking",
    )(u16, u16, u16, zz, row2(g_sb_head))

    out = pl.pallas_call(
        functools.partial(_out_kernel, heads=hc, dh=dc // hc, dm=dm, ds=ds),
        grid=(B, nt),
        in_specs=[rows(D, 0), rows(dm, 0), rows(ds, 0),
                  rows(dc, o16["q_c"]), rows(dc, ozz["z_c"]),
                  pl.BlockSpec((1, M, dc), lambda b, t: (b, 0, 0)),
                  pl.BlockSpec((1, dc, M), lambda b, t: (b, 0, 0)),
                  pl.BlockSpec((1, dc), const),
                  pl.BlockSpec((dm + ds + dc, D), const),
                  pl.BlockSpec((1, D), const)],
        out_specs=rows(D, 0),
        out_shape=jax.ShapeDtypeStruct((B * S, D), x.dtype),
        compiler_params=_params("arbitrary", "arbitrary"),
        name="mem_attn_out_proj",
    )(x2, y_m, y_s, u16, zz, k_c, vt_c, row2(g_mem_head), w_out.astype(BF16), row2(g_post))

    return out.reshape(B, S, D)
```

```python
import functools
import math

import jax
import jax.numpy as jnp
from jax import lax
from jax.experimental import pallas as pl
from jax.experimental.pallas import tpu as pltpu

F32 = jnp.float32
BF16 = jnp.bfloat16
EPS = 1e-6
LANES = 128
SUBLANES = 8
VMEM_LIMIT = 56 * 2 ** 20

MLSTM_HEADS = 4
SB_HEADS = 4
MEM_HEADS = 4
CONV_WIDTH = 4
CHUNK = 256
ROW_TILE = 1024
PROJ_TILE = 512
PROJ_ROWS = 256
PROJ_COLS = 256
PROJ_PIECES_PER_SLOT = 3
OUT_TAIL_BLOCKS = 2
SB_QTILE = 1024
SB_KTILE = 256
SB_STRIP = 32
SB_CUMSUM_TERMS = 1
LOG2E = 1.4426950408889634


def _sigmoid(x):
    return 1.0 / (1.0 + jnp.exp2(x * -LOG2E))


def _silu(x):
    return x * _sigmoid(x)


def _softplus(x):
    return jnp.maximum(x, 0.0) + jnp.log(1.0 + jnp.exp(-jnp.abs(x)))


def _dot(a, b):
    return jnp.dot(a, b, preferred_element_type=F32)


def _dot_nt(a, b):
    return lax.dot_general(a, b, (((1,), (1,)), ((), ())), preferred_element_type=F32)


def _split_bf16(x, parts):
    out, rem = [], x
    for n in range(parts):
        p = rem.astype(BF16)
        out.append(p)
        if n + 1 < parts:
            rem = rem - p.astype(F32)
    return out


def _rms(x, axis=-1):
    return x * lax.rsqrt(jnp.mean(x * x, axis=axis, keepdims=True) + EPS)


def _proj_mlstm_kernel(x_ref, gpre_ref, w_ref, cw_ref, cb_ref, gb_ref, gh_ref,
                       u16_ref, zz_ref, y_ref, qk_s, oz_s, v_s, gate_s, ct_state, m_state,
                       *, ts, nt, heads, dh, groups):
    g = pl.program_id(0)
    dm = heads * dh
    pad = SUBLANES
    ext = ct_state.shape[1]
    wslot = lax.rem(g, 2)
    rslot = 1 - wslot
    first = jnp.logical_or(g == 0, lax.rem(g - 1, nt) == 0)

    @pl.when(g == 0)
    def _():
        qk_s[...] = jnp.zeros_like(qk_s)
        oz_s[...] = jnp.zeros_like(oz_s)
        v_s[...] = jnp.zeros_like(v_s)
        gate_s[...] = jnp.zeros_like(gate_s)

    @pl.when(first)
    def _():
        ct_state[...] = jnp.zeros_like(ct_state)
        m_state[...] = jnp.zeros_like(m_state)

    tail = qk_s[wslot, ts:ts + pad, :]
    qk_s[rslot, 0:pad, :] = jnp.where(first, 0.0, tail)

    hb = {}
    pieces = [(r, name, c) for r in range(0, ts, PROJ_ROWS) for name in ("qk", "oz", "zz", "v", "u16", "gate")
              for c in range(0, groups[name][1], PROJ_COLS)]

    def project_slot():
        for _ in range(PROJ_PIECES_PER_SLOT):
            project_piece()

    def project_piece():
        if not pieces:
            return
        r, name, c = pieces.pop(0)
        if r not in hb:
            hb[r] = (_rms(x_ref[r:r + PROJ_ROWS, :]) * gpre_ref[...]).astype(BF16)
        start, width = groups[name]
        w = min(PROJ_COLS, width - c)
        res = _dot(hb[r], w_ref[:, start + c:start + c + w])
        rows, cs = slice(r, r + PROJ_ROWS), slice(c, c + w)
        if name == "qk":
            qk_s[wslot, pad + r:pad + r + PROJ_ROWS, cs] = res
        elif name == "oz":
            oz_s[wslot, rows, cs] = res
        elif name == "v":
            v_s[wslot, rows, cs] = res.astype(BF16)
        elif name == "gate":
            gate_s[wslot, rows, cs] = res
        elif name == "u16":
            u16_ref[rows, cs] = res.astype(BF16)
        else:
            zz_ref[rows, cs] = res

    lane = lax.broadcasted_iota(jnp.int32, (CHUNK, LANES), 1)
    row = lax.broadcasted_iota(jnp.int32, (CHUNK, CHUNK), 0)
    col = lax.broadcasted_iota(jnp.int32, (CHUNK, CHUNK), 1)
    tril_bf = jnp.where(row >= col, 1.0, 0.0).astype(BF16)
    upper = row <= col
    ones_col = jnp.where(lane == 0, 1.0, 0.0).astype(BF16)
    hsl = [slice(h * dh, (h + 1) * dh) for h in range(heads)]

    for c in range(ts // CHUNK):
        r0 = c * CHUNK
        acc = None
        for j in range(CONV_WIDTH):
            off = pad - (CONV_WIDTH - 1) + j + r0
            term = qk_s[rslot, off:off + CHUNK, :] * cw_ref[j:j + 1, :]
            acc = term if acc is None else acc + term
        qk = _silu(acc + cb_ref[...])
        project_slot()

        gt = gate_s[rslot, r0:r0 + CHUNK, :] + gb_ref[...]
        logf = -_softplus(-gt)
        cum = _dot(tril_bf, jnp.concatenate(_split_bf16(logf, 3), axis=1))
        bcum = cum[:, :LANES] + cum[:, LANES:2 * LANES] + cum[:, 2 * LANES:]
        p = jnp.where(lane < heads, gt, bcum)
        pt = p.T
        t = [dict() for _ in range(heads)]

        def operands(h):
            e = t[h]
            e["kb"] = (qk[:, dm + h * dh:dm + (h + 1) * dh] * (dh ** -0.5)).astype(BF16)
            e["qt"] = qk[:, hsl[h]].astype(BF16).T
            vext = jnp.concatenate([v_s[rslot, r0:r0 + CHUNK, hsl[h]], ones_col], axis=1)
            e["vt"] = vext.T[0:ext, :]
            e["kq"] = _dot(e["kb"], e["qt"])
            e["ct"] = ct_state[h]
            e["cq"] = _dot(e["ct"].astype(BF16), e["qt"])

        def decay_matrix(h):
            e = t[h]
            e["brow"], e["irow"] = pt[heads + h:heads + h + 1, :], pt[h:h + 1, :]
            cvec = p[:, h:h + 1] - p[:, heads + h:heads + h + 1]
            e["m_prev"] = m_state[h:h + 1, 0:1]
            dt = jnp.where(upper, cvec + e["brow"], -jnp.inf)
            inter = e["brow"] + e["m_prev"]
            e["m_t"] = jnp.maximum(inter, jnp.max(dt, axis=0, keepdims=True))
            e["inter_w"] = jnp.exp(inter - e["m_t"])
            st = e.pop("kq") * jnp.exp(dt - e["m_t"])
            e["sv"] = _dot(e["vt"], st.astype(BF16))

        def state_update(h):
            e = t[h]
            b_last = e["brow"][:, CHUNK - 1:CHUNK]
            a_end = b_last - e["brow"] + e["irow"]
            m_new = jnp.maximum(b_last + e["m_prev"], jnp.max(a_end, axis=1, keepdims=True))
            vw = (e["vt"] * jnp.exp(a_end - m_new)).astype(BF16)
            decay = jnp.exp(b_last + e["m_prev"] - m_new)
            ct_state[h] = decay * e["ct"] + _dot(vw, e["kb"])
            m_state[h:h + 1, :] = jnp.broadcast_to(m_new, (1, LANES))

        def output(h):
            e = t[h]
            ne = e["sv"] + e["inter_w"] * e["cq"]
            den = ne[dh:dh + 1, :]
            ht = ne[0:dh, :] / jnp.maximum(jnp.abs(den), jnp.exp(-e["m_t"]))
            hn = _rms(ht, axis=0).T * gh_ref[:, hsl[h]]
            y_ref[r0:r0 + CHUNK, hsl[h]] = (_sigmoid(oz_s[rslot, r0:r0 + CHUNK, hsl[h]]) * hn
                                            * _silu(oz_s[rslot, r0:r0 + CHUNK, dm + h * dh:dm + (h + 1) * dh]))
            e.clear()

        stages = (operands, decay_matrix, state_update, output)
        for step in range(heads + len(stages) - 1):
            for lag, stage in enumerate(stages):
                if 0 <= step - lag < heads:
                    stage(step - lag)
            project_slot()

    assert not pieces


def _memkv_kernel(mem_ref, g_ref, w_ref, k_ref, vt_ref):
    h = (_rms(mem_ref[0]) * g_ref[...]).astype(BF16)
    kv = _dot(h, w_ref[...])
    dc = kv.shape[1] // 2
    k_ref[0] = kv[:, :dc].astype(BF16)
    vt_ref[0] = kv[:, dc:].T.astype(BF16)


def _sb_kernel(q_ref, k_ref, v_ref, z_ref, g_ref, y_ref, acc_ref, carry_ref, pend_ref,
               *, tq, tk, heads, dh):
    i = pl.program_id(1)
    nsub = tq // tk
    jj = lax.broadcasted_iota(jnp.int32, (SB_CUMSUM_TERMS * tk, tk), 0) % tk
    ss = lax.broadcasted_iota(jnp.int32, (SB_CUMSUM_TERMS * tk, tk), 1)
    uu = jnp.where(jj >= ss, 1.0, 0.0).astype(BF16)
    strict = (lax.broadcasted_iota(jnp.int32, (tk, tk), 1)
              < lax.broadcasted_iota(jnp.int32, (tk, tk), 0))
    hs = [slice(h * dh, (h + 1) * dh) for h in range(heads)]
    strips = range(0, tk, SB_STRIP)

    def sweep(base, pairs, fresh, pending, drain):
        items = [(sub, koff, masked, h) for sub, koff, masked in pairs for h in range(heads)]
        z, hl, c = {}, {}, {}
        if pending:
            items = items + [(nsub - 1, 1, False, heads - 2), (nsub - 1, 1, False, heads - 1)]
            z[-2], c[-2], z[-1] = pend_ref[0], pend_ref[1], pend_ref[2]

        def kv_tile(ref, koff, h):
            k0 = (base + koff) * tk
            k0 = k0 if isinstance(k0, int) else pl.multiple_of(k0, tk)
            return ref[pl.ds(k0, tk), hs[h]]

        def scores(n):
            sub, koff, _, h = items[n]
            z[n] = _dot_nt(q_ref[sub * tk:(sub + 1) * tk, hs[h]], kv_tile(k_ref, koff, h))

        def softplus_split(n):
            parts = []
            for r in strips:
                zs = z[n][r:r + SB_STRIP]
                sp = jnp.maximum(zs, 0.0) + jnp.log(1.0 + jnp.exp2(jnp.abs(zs) * -LOG2E))
                if items[n][2]:
                    sp = jnp.where(strict[r:r + SB_STRIP], sp, 0.0)
                parts.append(jnp.concatenate(_split_bf16(sp, SB_CUMSUM_TERMS), axis=1))
            hl[n] = jnp.concatenate(parts, axis=0)

        def cumsum(n):
            c[n] = _dot(hl.pop(n), uu)

        def weights_times_v(n):
            sub, koff, masked, h = items[n]
            zn, cn = z.pop(n), c.pop(n)
            parts = []
            for r in strips:
                a = jnp.exp2((zn[r:r + SB_STRIP] - cn[r:r + SB_STRIP]) * LOG2E)
                if masked:
                    a = jnp.where(strict[r:r + SB_STRIP], a, 0.0)
                parts.append(a.astype(BF16))
            av = _dot(jnp.concatenate(parts, axis=0), kv_tile(v_ref, koff, h))
            slot = sub * heads + h
            if (sub, h) in fresh:
                fresh.discard((sub, h))
                carry_ref[slot] = cn[:, 0:1]
                acc_ref[slot] = av
            else:
                carry = carry_ref[slot]
                carry_ref[slot] = carry + cn[:, 0:1]
                acc_ref[slot] = acc_ref[slot] + jnp.exp2(carry * -LOG2E) * av

        n_new = len(items) - (2 if pending else 0)
        first = -2 if pending else 0
        for step in range(n_new + (2 if drain else 0)):
            if step < n_new:
                scores(step)
            if max(first, -1) <= step - 1 < n_new:
                softplus_split(step - 1)
                cumsum(step - 1)
            if first <= step - 2 < n_new:
                weights_times_v(step - 2)
        if not drain:
            pend_ref[0], pend_ref[1], pend_ref[2] = z[n_new - 2], c[n_new - 2], z[n_new - 1]

    diag = [(sub, koff, koff == sub) for sub in range(nsub) for koff in range(sub, -1, -1)]
    full = [(sub, 0, False) for sub in range(nsub)]
    fresh = {(sub, h) for sub in range(nsub) for h in range(heads)}
    sweep(i * nsub, diag, fresh, False, False)
    for sub, h in sorted(fresh):
        carry_ref[sub * heads + h] = jnp.zeros((tk, 1), F32)
        acc_ref[sub * heads + h] = jnp.zeros((tk, dh), F32)

    def body(n, _):
        sweep(i * nsub - 1 - n, full, set(), True, False)
        return 0

    lax.fori_loop(0, i * nsub, body, 0)
    sweep(-1, [], set(), True, True)

    for sub in range(nsub):
        hn = jnp.concatenate([_rms(acc_ref[sub * heads + h]) for h in range(heads)], axis=1)
        rows = slice(sub * tk, (sub + 1) * tk)
        y_ref[rows, :] = hn * g_ref[...] * _silu(z_ref[rows, :])


def _out_kernel(x_ref, ym_ref, ys_ref, qc_ref, zc_ref, k_ref, vt_ref, gmh_ref, wo_ref, gp_ref,
                out_ref, *, heads, dh, dm, ds):
    qt = qc_ref[...].T
    hs = [slice(h * dh, (h + 1) * dh) for h in range(heads)]
    sc, pr, den, o = {}, {}, {}, {}

    def scores(h):
        sc[h] = _dot(k_ref[0, :, hs[h]], qt[hs[h], :])

    def softmax(h):
        pr[h] = jnp.exp(sc[h] - jnp.max(sc[h], axis=0, keepdims=True))
        den[h] = jnp.sum(pr[h], axis=0, keepdims=True)
        o[h] = _dot(vt_ref[0, hs[h], :], pr[h].astype(BF16))

    def norm(h):
        o[h] = _rms(o[h] / den[h], axis=0)

    ymb, ysb = ym_ref[...].astype(BF16), ys_ref[...].astype(BF16)
    width = out_ref.shape[1] // heads

    def project(n):
        cs = slice(n * width, (n + 1) * width)
        out_ref[:, cs] = _dot(ymb, wo_ref[0:dm, cs]) + _dot(ysb, wo_ref[dm:dm + ds, cs])

    for step in range(heads + 2):
        for stage, lag in ((scores, 0), (softmax, 1), (norm, 2)):
            if 0 <= step - lag < heads:
                stage(step - lag)
        if 0 <= step - 1 < heads:
            project(step - 1)
    yc = jnp.concatenate([o[h] for h in range(heads)], axis=0).T * gmh_ref[...] * _silu(zc_ref[...])
    ycb = yc.astype(BF16)
    blk = x_ref.shape[0] // OUT_TAIL_BLOCKS
    for r in range(0, x_ref.shape[0], blk):
        yr = out_ref[r:r + blk, :] + _dot(ycb[r:r + blk], wo_ref[dm + ds:, :])
        out_ref[r:r + blk, :] = x_ref[r:r + blk, :] + _rms(yr) * gp_ref[...]


def _params(*sem):
    return pltpu.CompilerParams(dimension_semantics=sem, vmem_limit_bytes=VMEM_LIMIT)


def _col(width, offset):
    assert offset % width == 0
    return offset // width


def kernel(x, mem, g_pre, w_in, conv_w, conv_b, i_bias, f_bias, g_mlstm_head, g_sb_head,
           g_mem_head, g_mem, w_mem_kv, w_out, g_post):
    B, S, D = x.shape
    M = mem.shape[1]
    dm, ds, dc = g_mlstm_head.shape[0], g_sb_head.shape[0], g_mem_head.shape[0]
    hm, hsb, hc = MLSTM_HEADS, SB_HEADS, MEM_HEADS
    ts = ROW_TILE
    assert S % ts == 0 and ts % CHUNK == 0 and 2 * hm <= LANES
    nt = S // ts

    cols, off = {}, 0
    for name, width in (("q_m", dm), ("k_m", dm), ("v_m", dm), ("o_m", dm), ("z_m", dm),
                        ("if", 2 * hm), ("q_s", ds), ("k_s", ds), ("v_s", ds), ("z_s", ds),
                        ("q_c", dc), ("z_c", dc)):
        cols[name] = w_in[:, off:off + width]
        off += width
    sb_scale, mem_scale = (ds // hsb) ** -0.5, (dc // hc) ** -0.5
    assert math.log2(sb_scale).is_integer() and math.log2(mem_scale).is_integer()
    parts = (("qk", [cols["q_m"], cols["k_m"]]), ("oz", [cols["o_m"], cols["z_m"]]),
             ("zz", [cols["z_s"], cols["z_c"]]), ("v", [cols["v_m"]]),
             ("u16", [cols["q_s"] * sb_scale, cols["k_s"], cols["v_s"], cols["q_c"] * mem_scale]),
             ("gate", [jnp.pad(cols["if"], ((0, 0), (0, LANES - 2 * hm)))]))
    groups, start = {}, 0
    for name, arrs in parts:
        width = sum(a.shape[1] for a in arrs)
        groups[name] = (start, width)
        start += width
    w_cat = jnp.concatenate([a for _, arrs in parts for a in arrs], axis=1).astype(BF16)
    gate_bias = jnp.pad(jnp.concatenate([i_bias, f_bias]), (0, LANES - 2 * hm)).reshape(1, LANES)
    o16 = {"q_s": 0, "k_s": ds, "v_s": 2 * ds, "q_c": 3 * ds}
    ozz = {"z_s": 0, "z_c": ds}
    row2 = lambda a: a.reshape(1, -1)

    x2 = x.reshape(B * S, D)
    const = lambda *_: (0, 0)

    def rows(width, offset):
        c = _col(width, offset)
        return pl.BlockSpec((ts, width), lambda b, t: (b * nt + t, c))

    tp = PROJ_TILE
    assert S % tp == 0 and tp % CHUNK == 0 and tp % PROJ_ROWS == 0
    n_tiles = B * S // tp
    cur = lambda width: pl.BlockSpec((tp, width), lambda g: (jnp.minimum(g, n_tiles - 1), 0))
    u16, zz, y_m = pl.pallas_call(
        functools.partial(_proj_mlstm_kernel, ts=tp, nt=S // tp, heads=hm, dh=dm // hm, groups=groups),
        grid=(n_tiles + 1,),
        in_specs=[cur(D),
                  pl.BlockSpec((1, D), const),
                  pl.BlockSpec((D, w_cat.shape[1]), const),
                  pl.BlockSpec((CONV_WIDTH, 2 * dm), const),
                  pl.BlockSpec((1, 2 * dm), const),
                  pl.BlockSpec((1, LANES), const),
                  pl.BlockSpec((1, dm), const)],
        out_specs=[cur(groups["u16"][1]), cur(groups["zz"][1]),
                   pl.BlockSpec((tp, dm), lambda g: (jnp.maximum(g - 1, 0), 0))],
        out_shape=[jax.ShapeDtypeStruct((B * S, groups["u16"][1]), BF16),
                   jax.ShapeDtypeStruct((B * S, groups["zz"][1]), F32),
                   jax.ShapeDtypeStruct((B * S, dm), F32)],
        scratch_shapes=[pltpu.VMEM((2, tp + SUBLANES, 2 * dm), F32),
                        pltpu.VMEM((2, tp, 2 * dm), F32),
                        pltpu.VMEM((2, tp, dm), BF16),
                        pltpu.VMEM((2, tp, LANES), F32),
                        pltpu.VMEM((hm, dm // hm + 2 * SUBLANES, dm // hm), F32),
                        pltpu.VMEM((SUBLANES, LANES), F32)],
        compiler_params=_params("arbitrary"),
        name="proj_mlstm",
    )(x2, row2(g_pre), w_cat, conv_w, row2(conv_b), gate_bias, row2(g_mlstm_head))

    k_c, vt_c = pl.pallas_call(
        _memkv_kernel,
        grid=(B,),
        in_specs=[pl.BlockSpec((1, M, D), lambda b: (b, 0, 0)),
                  pl.BlockSpec((1, D), const),
                  pl.BlockSpec((D, 2 * dc), const)],
        out_specs=[pl.BlockSpec((1, M, dc), lambda b: (b, 0, 0)),
                   pl.BlockSpec((1, dc, M), lambda b: (b, 0, 0))],
        out_shape=[jax.ShapeDtypeStruct((B, M, dc), BF16),
                   jax.ShapeDtypeStruct((B, dc, M), BF16)],
        compiler_params=_params("arbitrary"),
        name="mem_kv",
    )(mem, row2(g_mem), w_mem_kv.astype(BF16))

    tq = SB_QTILE
    assert S % tq == 0 and tq % SB_KTILE == 0 and SB_KTILE % SB_STRIP == 0
    ntq = S // tq

    def qrows(width, offset):
        c = _col(width, offset)
        return pl.BlockSpec((tq, width), lambda b, t: (b * ntq + t, c))

    y_s = pl.pallas_call(
        functools.partial(_sb_kernel, tq=tq, tk=SB_KTILE, heads=hsb, dh=ds // hsb),
        grid=(B, ntq),
        in_specs=[qrows(ds, o16["q_s"]),
                  pl.BlockSpec((S, ds), lambda b, t: (b, _col(ds, o16["k_s"]))),
                  pl.BlockSpec((S, ds), lambda b, t: (b, _col(ds, o16["v_s"]))),
                  qrows(ds, ozz["z_s"]),
                  pl.BlockSpec((1, ds), const)],
        out_specs=qrows(ds, 0),
        out_shape=jax.ShapeDtypeStruct((B * S, ds), F32),
        scratch_shapes=[pltpu.VMEM((tq // SB_KTILE * hsb, SB_KTILE, ds // hsb), F32),
                        pltpu.VMEM((tq // SB_KTILE * hsb, SB_KTILE, 1), F32),
                        pltpu.VMEM((3, SB_KTILE, SB_KTILE), F32)],
        compiler_params=_params("arbitrary", "arbitrary"),
        name="stick_breaking",
    )(u16, u16, u16, zz, row2(g_sb_head))

    out = pl.pallas_call(
        functools.partial(_out_kernel, heads=hc, dh=dc // hc, dm=dm, ds=ds),
        grid=(B, nt),
        in_specs=[rows(D, 0), rows(dm, 0), rows(ds, 0),
                  rows(dc, o16["q_c"]), rows(dc, ozz["z_c"]),
                  pl.BlockSpec((1, M, dc), lambda b, t: (b, 0, 0)),
                  pl.BlockSpec((1, dc, M), lambda b, t: (b, 0, 0)),
                  pl.BlockSpec((1, dc), const),
                  pl.BlockSpec((dm + ds + dc, D), const),
                  pl.BlockSpec((1, D), const)],
        out_specs=rows(D, 0),
        out_shape=jax.ShapeDtypeStruct((B * S, D), x.dtype),
        compiler_params=_params("arbitrary", "arbitrary"),
        name="mem_attn_out_proj",
    )(x2, y_m, y_s, u16, zz, k_c, vt_c, row2(g_mem_head), w_out.astype(BF16), row2(g_post))

    return out.reshape(B, S, D)
```

```python
import functools
import math

import jax
import jax.numpy as jnp
from jax import lax
from jax.experimental import pallas as pl
from jax.experimental.pallas import tpu as pltpu

F32 = jnp.float32
BF16 = jnp.bfloat16
EPS = 1e-6
LANES = 128
SUBLANES = 8
VMEM_LIMIT = 56 * 2 ** 20

MLSTM_HEADS = 4
SB_HEADS = 4
MEM_HEADS = 4
CONV_WIDTH = 4
CHUNK = 256
ROW_TILE = 1024
PROJ_TILE = 512
PROJ_ROWS = 256
PROJ_COLS = 256
PROJ_PIECES_PER_SLOT = 3
MEMKV_BATCH = 4
OUT_TAIL_BLOCKS = 2
SB_QTILE = 1024
SB_KTILE = 256
SB_STRIP = 32
SB_CUMSUM_TERMS = 1
LOG2E = 1.4426950408889634


def _sigmoid(x):
    return 1.0 / (1.0 + jnp.exp2(x * -LOG2E))


def _silu(x):
    return x * _sigmoid(x)


def _softplus(x):
    return jnp.maximum(x, 0.0) + jnp.log(1.0 + jnp.exp(-jnp.abs(x)))


def _dot(a, b):
    return jnp.dot(a, b, preferred_element_type=F32)


def _dot_nt(a, b):
    return lax.dot_general(a, b, (((1,), (1,)), ((), ())), preferred_element_type=F32)


def _split_bf16(x, parts):
    out, rem = [], x
    for n in range(parts):
        p = rem.astype(BF16)
        out.append(p)
        if n + 1 < parts:
            rem = rem - p.astype(F32)
    return out


def _rms(x, axis=-1):
    return x * lax.rsqrt(jnp.mean(x * x, axis=axis, keepdims=True) + EPS)


def _proj_mlstm_kernel(x_ref, gpre_ref, w_ref, cw_ref, cb_ref, gb_ref, gh_ref,
                       u16_ref, zz_ref, y_ref, qk_s, oz_s, v_s, gate_s, ct_state, m_state,
                       *, ts, nt, heads, dh, groups):
    g = pl.program_id(0)
    dm = heads * dh
    pad = SUBLANES
    ext = ct_state.shape[1]
    wslot = lax.rem(g, 2)
    rslot = 1 - wslot
    first = jnp.logical_or(g == 0, lax.rem(g - 1, nt) == 0)

    @pl.when(g == 0)
    def _():
        qk_s[...] = jnp.zeros_like(qk_s)
        oz_s[...] = jnp.zeros_like(oz_s)
        v_s[...] = jnp.zeros_like(v_s)
        gate_s[...] = jnp.zeros_like(gate_s)

    @pl.when(first)
    def _():
        ct_state[...] = jnp.zeros_like(ct_state)
        m_state[...] = jnp.zeros_like(m_state)

    tail = qk_s[wslot, ts:ts + pad, :]
    qk_s[rslot, 0:pad, :] = jnp.where(first, 0.0, tail)

    hb = {}
    pieces = [(r, name, c) for r in range(0, ts, PROJ_ROWS) for name in ("qk", "oz", "zz", "v", "u16", "gate")
              for c in range(0, groups[name][1], PROJ_COLS)]

    def project_slot():
        for _ in range(PROJ_PIECES_PER_SLOT):
            project_piece()

    def project_piece():
        if not pieces:
            return
        r, name, c = pieces.pop(0)
        if r not in hb:
            hb[r] = (_rms(x_ref[r:r + PROJ_ROWS, :]) * gpre_ref[...]).astype(BF16)
        start, width = groups[name]
        w = min(PROJ_COLS, width - c)
        res = _dot(hb[r], w_ref[:, start + c:start + c + w])
        rows, cs = slice(r, r + PROJ_ROWS), slice(c, c + w)
        if name == "qk":
            qk_s[wslot, pad + r:pad + r + PROJ_ROWS, cs] = res
        elif name == "oz":
            oz_s[wslot, rows, cs] = res
        elif name == "v":
            v_s[wslot, rows, cs] = res.astype(BF16)
        elif name == "gate":
            gate_s[wslot, rows, cs] = res
        elif name == "u16":
            u16_ref[rows, cs] = res.astype(BF16)
        else:
            zz_ref[rows, cs] = res

    lane = lax.broadcasted_iota(jnp.int32, (CHUNK, LANES), 1)
    row = lax.broadcasted_iota(jnp.int32, (CHUNK, CHUNK), 0)
    col = lax.broadcasted_iota(jnp.int32, (CHUNK, CHUNK), 1)
    tril_bf = jnp.where(row >= col, 1.0, 0.0).astype(BF16)
    upper = row <= col
    ones_col = jnp.where(lane == 0, 1.0, 0.0).astype(BF16)
    hsl = [slice(h * dh, (h + 1) * dh) for h in range(heads)]

    for c in range(ts // CHUNK):
        r0 = c * CHUNK
        acc = None
        for j in range(CONV_WIDTH):
            off = pad - (CONV_WIDTH - 1) + j + r0
            term = qk_s[rslot, off:off + CHUNK, :] * cw_ref[j:j + 1, :]
            acc = term if acc is None else acc + term
        qk = _silu(acc + cb_ref[...])
        project_slot()

        gt = gate_s[rslot, r0:r0 + CHUNK, :] + gb_ref[...]
        logf = -_softplus(-gt)
        cum = _dot(tril_bf, jnp.concatenate(_split_bf16(logf, 3), axis=1))
        bcum = cum[:, :LANES] + cum[:, LANES:2 * LANES] + cum[:, 2 * LANES:]
        p = jnp.where(lane < heads, gt, bcum)
        pt = p.T
        t = [dict() for _ in range(heads)]

        def operands(h):
            e = t[h]
            e["kb"] = (qk[:, dm + h * dh:dm + (h + 1) * dh] * (dh ** -0.5)).astype(BF16)
            e["qt"] = qk[:, hsl[h]].astype(BF16).T
            vext = jnp.concatenate([v_s[rslot, r0:r0 + CHUNK, hsl[h]], ones_col], axis=1)
            e["vt"] = vext.T[0:ext, :]
            e["kq"] = _dot(e["kb"], e["qt"])
            e["ct"] = ct_state[h]
            e["cq"] = _dot(e["ct"].astype(BF16), e["qt"])

        def decay_matrix(h):
            e = t[h]
            e["brow"], e["irow"] = pt[heads + h:heads + h + 1, :], pt[h:h + 1, :]
            cvec = p[:, h:h + 1] - p[:, heads + h:heads + h + 1]
            e["m_prev"] = m_state[h:h + 1, 0:1]
            dt = jnp.where(upper, cvec + e["brow"], -jnp.inf)
            inter = e["brow"] + e["m_prev"]
            e["m_t"] = jnp.maximum(inter, jnp.max(dt, axis=0, keepdims=True))
            e["inter_w"] = jnp.exp(inter - e["m_t"])
            st = e.pop("kq") * jnp.exp(dt - e["m_t"])
            e["sv"] = _dot(e["vt"], st.astype(BF16))

        def state_update(h):
            e = t[h]
            b_last = e["brow"][:, CHUNK - 1:CHUNK]
            a_end = b_last - e["brow"] + e["irow"]
            m_new = jnp.maximum(b_last + e["m_prev"], jnp.max(a_end, axis=1, keepdims=True))
            vw = (e["vt"] * jnp.exp(a_end - m_new)).astype(BF16)
            decay = jnp.exp(b_last + e["m_prev"] - m_new)
            ct_state[h] = decay * e["ct"] + _dot(vw, e["kb"])
            m_state[h:h + 1, :] = jnp.broadcast_to(m_new, (1, LANES))

        def output(h):
            e = t[h]
            ne = e["sv"] + e["inter_w"] * e["cq"]
            den = ne[dh:dh + 1, :]
            ht = ne[0:dh, :] / jnp.maximum(jnp.abs(den), jnp.exp(-e["m_t"]))
            hn = _rms(ht, axis=0).T * gh_ref[:, hsl[h]]
            y = (_sigmoid(oz_s[rslot, r0:r0 + CHUNK, hsl[h]]) * hn
                 * _silu(oz_s[rslot, r0:r0 + CHUNK, dm + h * dh:dm + (h + 1) * dh]))
            y_ref[r0:r0 + CHUNK, hsl[h]] = y.astype(BF16)
            e.clear()

        stages = (operands, decay_matrix, state_update, output)
        for step in range(heads + len(stages) - 1):
            for lag, stage in enumerate(stages):
                if 0 <= step - lag < heads:
                    stage(step - lag)
            project_slot()

    assert not pieces


def _memkv_kernel(mem_ref, g_ref, w_ref, k_ref, vt_ref):
    for b in range(mem_ref.shape[0]):
        h = (_rms(mem_ref[b]) * g_ref[...]).astype(BF16)
        kv = _dot(h, w_ref[...])
        dc = kv.shape[1] // 2
        k_ref[b] = kv[:, :dc].astype(BF16)
        vt_ref[b] = kv[:, dc:].T.astype(BF16)


def _sb_kernel(q_ref, k_ref, v_ref, z_ref, g_ref, y_ref, acc_ref, carry_ref, pend_ref,
               *, tq, tk, heads, dh):
    i = pl.program_id(1)
    nsub = tq // tk
    jj = lax.broadcasted_iota(jnp.int32, (SB_CUMSUM_TERMS * tk, tk), 0) % tk
    ss = lax.broadcasted_iota(jnp.int32, (SB_CUMSUM_TERMS * tk, tk), 1)
    uu = jnp.where(jj >= ss, 1.0, 0.0).astype(BF16)
    strict = (lax.broadcasted_iota(jnp.int32, (tk, tk), 1)
              < lax.broadcasted_iota(jnp.int32, (tk, tk), 0))
    hs = [slice(h * dh, (h + 1) * dh) for h in range(heads)]
    strips = range(0, tk, SB_STRIP)

    def sweep(base, pairs, fresh, pending, drain):
        items = [(sub, koff, masked, h) for sub, koff, masked in pairs for h in range(heads)]
        z, hl, c = {}, {}, {}
        if pending:
            items = items + [(nsub - 1, 1, False, heads - 2), (nsub - 1, 1, False, heads - 1)]
            z[-2], c[-2], z[-1] = pend_ref[0], pend_ref[1], pend_ref[2]

        def kv_tile(ref, koff, h):
            k0 = (base + koff) * tk
            k0 = k0 if isinstance(k0, int) else pl.multiple_of(k0, tk)
            return ref[pl.ds(k0, tk), hs[h]]

        def scores(n):
            sub, koff, _, h = items[n]
            z[n] = _dot_nt(q_ref[sub * tk:(sub + 1) * tk, hs[h]], kv_tile(k_ref, koff, h))

        def softplus_split(n):
            parts = []
            for r in strips:
                zs = z[n][r:r + SB_STRIP]
                sp = jnp.maximum(zs, 0.0) + jnp.log(1.0 + jnp.exp2(jnp.abs(zs) * -LOG2E))
                if items[n][2]:
                    sp = jnp.where(strict[r:r + SB_STRIP], sp, 0.0)
                parts.append(jnp.concatenate(_split_bf16(sp, SB_CUMSUM_TERMS), axis=1))
            hl[n] = jnp.concatenate(parts, axis=0)

        def cumsum(n):
            c[n] = _dot(hl.pop(n), uu)

        def weights_times_v(n):
            sub, koff, masked, h = items[n]
            zn, cn = z.pop(n), c.pop(n)
            parts = []
            for r in strips:
                a = jnp.exp2((zn[r:r + SB_STRIP] - cn[r:r + SB_STRIP]) * LOG2E)
                if masked:
                    a = jnp.where(strict[r:r + SB_STRIP], a, 0.0)
                parts.append(a.astype(BF16))
            av = _dot(jnp.concatenate(parts, axis=0), kv_tile(v_ref, koff, h))
            slot = sub * heads + h
            if (sub, h) in fresh:
                fresh.discard((sub, h))
                carry_ref[slot] = cn[:, 0:1]
                acc_ref[slot] = av
            else:
                carry = carry_ref[slot]
                carry_ref[slot] = carry + cn[:, 0:1]
                acc_ref[slot] = acc_ref[slot] + jnp.exp2(carry * -LOG2E) * av

        n_new = len(items) - (2 if pending else 0)
        first = -2 if pending else 0
        for step in range(n_new + (2 if drain else 0)):
            if step < n_new:
                scores(step)
            if max(first, -1) <= step - 1 < n_new:
                softplus_split(step - 1)
                cumsum(step - 1)
            if first <= step - 2 < n_new:
                weights_times_v(step - 2)
        if not drain:
            pend_ref[0], pend_ref[1], pend_ref[2] = z[n_new - 2], c[n_new - 2], z[n_new - 1]

    diag = [(sub, koff, koff == sub) for sub in range(nsub) for koff in range(sub, -1, -1)]
    full = [(sub, 0, False) for sub in range(nsub)]
    fresh = {(sub, h) for sub in range(nsub) for h in range(heads)}
    sweep(i * nsub, diag, fresh, False, False)
    for sub, h in sorted(fresh):
        carry_ref[sub * heads + h] = jnp.zeros((tk, 1), F32)
        acc_ref[sub * heads + h] = jnp.zeros((tk, dh), F32)

    def body(n, _):
        sweep(i * nsub - 1 - n, full, set(), True, False)
        return 0

    lax.fori_loop(0, i * nsub, body, 0)
    sweep(-1, [], set(), True, True)

    for sub in range(nsub):
        hn = jnp.concatenate([_rms(acc_ref[sub * heads + h]) for h in range(heads)], axis=1)
        rows = slice(sub * tk, (sub + 1) * tk)
        y_ref[rows, :] = (hn * g_ref[...] * _silu(z_ref[rows, :])).astype(BF16)


def _out_kernel(x_ref, ym_ref, ys_ref, qc_ref, zc_ref, k_ref, vt_ref, gmh_ref, wo_ref, gp_ref,
                out_ref, *, heads, dh, dm, ds):
    qt = qc_ref[...].T
    hs = [slice(h * dh, (h + 1) * dh) for h in range(heads)]
    sc, pr, den, o = {}, {}, {}, {}

    def scores(h):
        sc[h] = _dot(k_ref[0, :, hs[h]], qt[hs[h], :])

    def softmax(h):
        pr[h] = jnp.exp(sc[h] - jnp.max(sc[h], axis=0, keepdims=True))
        den[h] = jnp.sum(pr[h], axis=0, keepdims=True)
        o[h] = _dot(vt_ref[0, hs[h], :], pr[h].astype(BF16))

    def norm(h):
        o[h] = _rms(o[h] / den[h], axis=0)

    ymb, ysb = ym_ref[...], ys_ref[...]
    width = out_ref.shape[1] // heads

    def project(n):
        cs = slice(n * width, (n + 1) * width)
        out_ref[:, cs] = _dot(ymb, wo_ref[0:dm, cs]) + _dot(ysb, wo_ref[dm:dm + ds, cs])

    for step in range(heads + 2):
        for stage, lag in ((scores, 0), (softmax, 1), (norm, 2)):
            if 0 <= step - lag < heads:
                stage(step - lag)
        if 0 <= step - 1 < heads:
            project(step - 1)
    yc = jnp.concatenate([o[h] for h in range(heads)], axis=0).T * gmh_ref[...] * _silu(zc_ref[...])
    ycb = yc.astype(BF16)
    blk = x_ref.shape[0] // OUT_TAIL_BLOCKS
    for r in range(0, x_ref.shape[0], blk):
        yr = out_ref[r:r + blk, :] + _dot(ycb[r:r + blk], wo_ref[dm + ds:, :])
        out_ref[r:r + blk, :] = x_ref[r:r + blk, :] + _rms(yr) * gp_ref[...]


def _params(*sem):
    return pltpu.CompilerParams(dimension_semantics=sem, vmem_limit_bytes=VMEM_LIMIT)


def _col(width, offset):
    assert offset % width == 0
    return offset // width


def kernel(x, mem, g_pre, w_in, conv_w, conv_b, i_bias, f_bias, g_mlstm_head, g_sb_head,
           g_mem_head, g_mem, w_mem_kv, w_out, g_post):
    B, S, D = x.shape
    M = mem.shape[1]
    dm, ds, dc = g_mlstm_head.shape[0], g_sb_head.shape[0], g_mem_head.shape[0]
    hm, hsb, hc = MLSTM_HEADS, SB_HEADS, MEM_HEADS
    ts = ROW_TILE
    assert S % ts == 0 and ts % CHUNK == 0 and 2 * hm <= LANES
    nt = S // ts

    cols, off = {}, 0
    for name, width in (("q_m", dm), ("k_m", dm), ("v_m", dm), ("o_m", dm), ("z_m", dm),
                        ("if", 2 * hm), ("q_s", ds), ("k_s", ds), ("v_s", ds), ("z_s", ds),
                        ("q_c", dc), ("z_c", dc)):
        cols[name] = w_in[:, off:off + width]
        off += width
    sb_scale, mem_scale = (ds // hsb) ** -0.5, (dc // hc) ** -0.5
    assert math.log2(sb_scale).is_integer() and math.log2(mem_scale).is_integer()
    parts = (("qk", [cols["q_m"], cols["k_m"]]), ("oz", [cols["o_m"], cols["z_m"]]),
             ("zz", [cols["z_s"], cols["z_c"]]), ("v", [cols["v_m"]]),
             ("u16", [cols["q_s"] * sb_scale, cols["k_s"], cols["v_s"], cols["q_c"] * mem_scale]),
             ("gate", [jnp.pad(cols["if"], ((0, 0), (0, LANES - 2 * hm)))]))
    groups, start = {}, 0
    for name, arrs in parts:
        width = sum(a.shape[1] for a in arrs)
        groups[name] = (start, width)
        start += width
    w_cat = jnp.concatenate([a for _, arrs in parts for a in arrs], axis=1).astype(BF16)
    gate_bias = jnp.pad(jnp.concatenate([i_bias, f_bias]), (0, LANES - 2 * hm)).reshape(1, LANES)
    o16 = {"q_s": 0, "k_s": ds, "v_s": 2 * ds, "q_c": 3 * ds}
    ozz = {"z_s": 0, "z_c": ds}
    row2 = lambda a: a.reshape(1, -1)

    x2 = x.reshape(B * S, D)
    const = lambda *_: (0, 0)

    def rows(width, offset):
        c = _col(width, offset)
        return pl.BlockSpec((ts, width), lambda b, t: (b * nt + t, c))

    tp = PROJ_TILE
    assert S % tp == 0 and tp % CHUNK == 0 and tp % PROJ_ROWS == 0
    n_tiles = B * S // tp
    cur = lambda width: pl.BlockSpec((tp, width), lambda g: (jnp.minimum(g, n_tiles - 1), 0))
    u16, zz, y_m = pl.pallas_call(
        functools.partial(_proj_mlstm_kernel, ts=tp, nt=S // tp, heads=hm, dh=dm // hm, groups=groups),
        grid=(n_tiles + 1,),
        in_specs=[cur(D),
                  pl.BlockSpec((1, D), const),
                  pl.BlockSpec((D, w_cat.shape[1]), const),
                  pl.BlockSpec((CONV_WIDTH, 2 * dm), const),
                  pl.BlockSpec((1, 2 * dm), const),
                  pl.BlockSpec((1, LANES), const),
                  pl.BlockSpec((1, dm), const)],
        out_specs=[cur(groups["u16"][1]), cur(groups["zz"][1]),
                   pl.BlockSpec((tp, dm), lambda g: (jnp.maximum(g - 1, 0), 0))],
        out_shape=[jax.ShapeDtypeStruct((B * S, groups["u16"][1]), BF16),
                   jax.ShapeDtypeStruct((B * S, groups["zz"][1]), F32),
                   jax.ShapeDtypeStruct((B * S, dm), BF16)],
        scratch_shapes=[pltpu.VMEM((2, tp + SUBLANES, 2 * dm), F32),
                        pltpu.VMEM((2, tp, 2 * dm), F32),
                        pltpu.VMEM((2, tp, dm), BF16),
                        pltpu.VMEM((2, tp, LANES), F32),
                        pltpu.VMEM((hm, dm // hm + 2 * SUBLANES, dm // hm), F32),
                        pltpu.VMEM((SUBLANES, LANES), F32)],
        compiler_params=_params("arbitrary"),
        name="proj_mlstm",
    )(x2, row2(g_pre), w_cat, conv_w, row2(conv_b), gate_bias, row2(g_mlstm_head))

    bb = math.gcd(B, MEMKV_BATCH)
    k_c, vt_c = pl.pallas_call(
        _memkv_kernel,
        grid=(B // bb,),
        in_specs=[pl.BlockSpec((bb, M, D), lambda b: (b, 0, 0)),
                  pl.BlockSpec((1, D), const),
                  pl.BlockSpec((D, 2 * dc), const)],
        out_specs=[pl.BlockSpec((bb, M, dc), lambda b: (b, 0, 0)),
                   pl.BlockSpec((bb, dc, M), lambda b: (b, 0, 0))],
        out_shape=[jax.ShapeDtypeStruct((B, M, dc), BF16),
                   jax.ShapeDtypeStruct((B, dc, M), BF16)],
        compiler_params=_params("arbitrary"),
        name="mem_kv",
    )(mem, row2(g_mem), w_mem_kv.astype(BF16))

    tq = SB_QTILE
    assert S % tq == 0 and tq % SB_KTILE == 0 and SB_KTILE % SB_STRIP == 0
    ntq = S // tq

    def qrows(width, offset):
        c = _col(width, offset)
        return pl.BlockSpec((tq, width), lambda b, t: (b * ntq + t, c))

    y_s = pl.pallas_call(
        functools.partial(_sb_kernel, tq=tq, tk=SB_KTILE, heads=hsb, dh=ds // hsb),
        grid=(B, ntq),
        in_specs=[qrows(ds, o16["q_s"]),
                  pl.BlockSpec((S, ds), lambda b, t: (b, _col(ds, o16["k_s"]))),
                  pl.BlockSpec((S, ds), lambda b, t: (b, _col(ds, o16["v_s"]))),
                  qrows(ds, ozz["z_s"]),
                  pl.BlockSpec((1, ds), const)],
        out_specs=qrows(ds, 0),
        out_shape=jax.ShapeDtypeStruct((B * S, ds), BF16),
        scratch_shapes=[pltpu.VMEM((tq // SB_KTILE * hsb, SB_KTILE, ds // hsb), F32),
                        pltpu.VMEM((tq // SB_KTILE * hsb, SB_KTILE, 1), F32),
                        pltpu.VMEM((3, SB_KTILE, SB_KTILE), F32)],
        compiler_params=_params("arbitrary", "arbitrary"),
        name="stick_breaking",
    )(u16, u16, u16, zz, row2(g_sb_head))

    out = pl.pallas_call(
        functools.partial(_out_kernel, heads=hc, dh=dc // hc, dm=dm, ds=ds),
        grid=(B, nt),
        in_specs=[rows(D, 0), rows(dm, 0), rows(ds, 0),
                  rows(dc, o16["q_c"]), rows(dc, ozz["z_c"]),
                  pl.BlockSpec((1, M, dc), lambda b, t: (b, 0, 0)),
                  pl.BlockSpec((1, dc, M), lambda b, t: (b, 0, 0)),
                  pl.BlockSpec((1, dc), const),
                  pl.BlockSpec((dm + ds + dc, D), const),
                  pl.BlockSpec((1, D), const)],
        out_specs=rows(D, 0),
        out_shape=jax.ShapeDtypeStruct((B * S, D), x.dtype),
        compiler_params=_params("arbitrary", "arbitrary"),
        name="mem_attn_out_proj",
    )(x2, y_m, y_s, u16, zz, k_c, vt_c, row2(g_mem_head), w_out.astype(BF16), row2(g_post))

    return out.reshape(B, S, D)
```
